```python
import jax, jax.numpy as jnp
from jax import lax
import numpy as np

D_MODEL = 1024
BATCH = 8
SEQ = 4096
DEPTH = 2

N_MIXERS = 2
N_POOL_GROUPS = 4
POOL_GROUP = D_MODEL // N_POOL_GROUPS
POOL_WINDOWS = (2, 4, 8, 16)
N_HEADS = 16
QK_NOPE = 64
QK_ROPE = 32
V_HEAD = 64
Q_LORA = D_MODEL // 4
KV_LORA = D_MODEL // 8
ROPE_THETA = 10000.0
D_FF = 11 * D_MODEL // 4
Q_BLOCK = 128
EPS = 1e-6
N_MOD = 9
N_POOL_LAYERS = (DEPTH + 1) // 2
N_MLA_LAYERS = DEPTH // 2
ATTN_SCALE = (QK_NOPE + QK_ROPE) ** -0.5

kernel_name = "hybrid_pool_mla_macaron_encoder"


def rmsnorm(x, g):
    xf = x.astype(jnp.float32)
    y = xf * lax.rsqrt(jnp.mean(xf * xf, axis=-1, keepdims=True) + EPS)
    return (y * g.astype(jnp.float32)).astype(x.dtype)


def swiglu(h, w_in, w_out):
    gate, up = jnp.split(h @ w_in, 2, axis=-1)
    return (jax.nn.silu(gate) * up) @ w_out


def centred_mean(x, window):
    s = x.shape[1]
    cs = lax.cumsum(x.astype(jnp.float32), axis=1)
    cs = jnp.pad(cs, ((0, 0), (1, 0), (0, 0)))
    t = jnp.arange(s)
    hi = jnp.clip(t + window // 2, 0, s)
    lo = jnp.clip(t - window // 2, 0, s)
    tot = jnp.take(cs, hi, axis=1) - jnp.take(cs, lo, axis=1)
    cnt = (hi - lo).astype(jnp.float32)[None, :, None]
    return (tot / cnt).astype(x.dtype)


def pool_mixer(h, w, b, scale):
    B, S, _ = h.shape
    hg = h.reshape(B, S, N_POOL_GROUPS, POOL_GROUP)
    pooled = jnp.stack([centred_mean(hg[:, :, g], POOL_WINDOWS[g]) for g in range(N_POOL_GROUPS)], axis=2)
    y = jnp.einsum('bsgc,gcd->bsgd', pooled - hg, w) + b
    return y.reshape(B, S, D_MODEL) * scale


def rope_tables(s, dtype):
    inv = 1.0 / (ROPE_THETA ** (jnp.arange(0, QK_ROPE, 2, dtype=jnp.float32) / QK_ROPE))
    ang = jnp.arange(s, dtype=jnp.float32)[:, None] * inv[None, :]
    return jnp.cos(ang).astype(dtype), jnp.sin(ang).astype(dtype)


def apply_rope(x, cos, sin):
    x1, x2 = jnp.split(x, 2, axis=-1)
    return jnp.concatenate([x1 * cos - x2 * sin, x2 * cos + x1 * sin], axis=-1)


def mla_mixer(h, w_in, q_norm, kv_norm, w_uq, w_uk, w_uv, w_o, cos, sin):
    B, S, _ = h.shape
    lat = h @ w_in
    c_q, c_kv, k_r = jnp.split(lat, [Q_LORA, Q_LORA + KV_LORA], axis=-1)
    c_q = rmsnorm(c_q, q_norm)
    c_kv = rmsnorm(c_kv, kv_norm)
    q = jnp.einsum('bsc,chd->bshd', c_q, w_uq)
    q_nope, q_rope = q[..., :QK_NOPE], q[..., QK_NOPE:]
    q_rope = apply_rope(q_rope, cos[:, None, :], sin[:, None, :]) * ATTN_SCALE
    k_rope = apply_rope(k_r, cos, sin)
    q_lat = jnp.einsum('bshn,chn->bshc', q_nope, w_uk) * ATTN_SCALE
    nb = S // Q_BLOCK
    qlb = q_lat.reshape(B, nb, Q_BLOCK, N_HEADS, KV_LORA).transpose(1, 0, 2, 3, 4)
    qrb = q_rope.reshape(B, nb, Q_BLOCK, N_HEADS, QK_ROPE).transpose(1, 0, 2, 3, 4)

    def block(args):
        ql, qr = args
        s = (jnp.einsum('bqhc,bkc->bhqk', ql, c_kv)
             + jnp.einsum('bqhr,bkr->bhqk', qr, k_rope))
        p = jax.nn.softmax(s.astype(jnp.float32), axis=-1).astype(c_kv.dtype)
        return jnp.einsum('bhqk,bkc->bqhc', p, c_kv)

    o_lat = lax.map(block, (qlb, qrb))
    o_lat = o_lat.transpose(1, 0, 2, 3, 4).reshape(B, S, N_HEADS, KV_LORA)
    o = jnp.einsum('bshc,chv->bshv', o_lat, w_uv)
    return o.reshape(B, S, N_HEADS * V_HEAD) @ w_o


def modulated_sublayer(x, mod, g_pre, g_post, fn, weight):
    shift, scale, gate = mod[:, 0], mod[:, 1], mod[:, 2]
    h = rmsnorm(x, g_pre) * (1.0 + scale) + shift
    y = rmsnorm(fn(h), g_post)
    return x + weight * (1.0 + gate) * y


def setup_inputs(seed: int = 0) -> dict:
    key = jax.random.key(seed)
    ks = jax.random.split(key, 20)
    n = jax.random.normal
    f32 = jnp.float32
    return {
        "x": n(ks[0], (BATCH, SEQ, D_MODEL), f32),
        "c": n(ks[1], (BATCH, D_MODEL), f32),
        "ada_w": n(ks[2], (DEPTH, D_MODEL, N_MOD * D_MODEL), f32) * (0.5 * D_MODEL ** -0.5),
        "ada_b": n(ks[3], (DEPTH, N_MOD * D_MODEL), f32) * 0.01,
        "norm_g": 1.0 + 0.05 * n(ks[4], (DEPTH, 6, D_MODEL), f32),
        "ffn_w_in": n(ks[5], (DEPTH, 2, D_MODEL, 2 * D_FF), f32) * D_MODEL ** -0.5,
        "ffn_w_out": n(ks[6], (DEPTH, 2, D_FF, D_MODEL), f32) * D_FF ** -0.5,
        "pool_w": n(ks[7], (N_POOL_LAYERS, N_POOL_GROUPS, POOL_GROUP, POOL_GROUP), f32) * POOL_GROUP ** -0.5,
        "pool_b": n(ks[8], (N_POOL_LAYERS, N_POOL_GROUPS, POOL_GROUP), f32) * 0.01,
        "pool_scale": 1.0 + 0.05 * n(ks[9], (N_POOL_LAYERS, D_MODEL), f32),
        "mla_w_in": n(ks[10], (N_MLA_LAYERS, D_MODEL, Q_LORA + KV_LORA + QK_ROPE), f32) * D_MODEL ** -0.5,
        "mla_q_norm": 1.0 + 0.05 * n(ks[11], (N_MLA_LAYERS, Q_LORA), f32),
        "mla_kv_norm": 1.0 + 0.05 * n(ks[12], (N_MLA_LAYERS, KV_LORA), f32),
        "mla_w_uq": n(ks[13], (N_MLA_LAYERS, Q_LORA, N_HEADS, QK_NOPE + QK_ROPE), f32) * Q_LORA ** -0.5,
        "mla_w_uk": n(ks[14], (N_MLA_LAYERS, KV_LORA, N_HEADS, QK_NOPE), f32) * KV_LORA ** -0.5,
        "mla_w_uv": n(ks[15], (N_MLA_LAYERS, KV_LORA, N_HEADS, V_HEAD), f32) * KV_LORA ** -0.5,
        "mla_w_o": n(ks[16], (N_MLA_LAYERS, N_HEADS * V_HEAD, D_MODEL), f32) * (N_HEADS * V_HEAD) ** -0.5,
    }


def reference(x, c, ada_w, ada_b, norm_g, ffn_w_in, ffn_w_out, pool_w, pool_b, pool_scale,
              mla_w_in, mla_q_norm, mla_kv_norm, mla_w_uq, mla_w_uk, mla_w_uv, mla_w_o):
    B = x.shape[0]
    cos, sin = rope_tables(x.shape[1], x.dtype)
    sc = jax.nn.silu(c)
    for i in range(DEPTH):
        mod = (sc @ ada_w[i] + ada_b[i]).reshape(B, N_MOD, D_MODEL)[:, :, None, :]
        g = norm_g[i]
        x = modulated_sublayer(x, mod[:, 0:3], g[0], g[1],
                               lambda h: swiglu(h, ffn_w_in[i, 0], ffn_w_out[i, 0]), 0.5)
        if i % N_MIXERS == 0:
            li = i // N_MIXERS
            mixer = lambda h: pool_mixer(h, pool_w[li], pool_b[li], pool_scale[li])
        else:
            li = i // N_MIXERS
            mixer = lambda h: mla_mixer(h, mla_w_in[li], mla_q_norm[li], mla_kv_norm[li],
                                        mla_w_uq[li], mla_w_uk[li], mla_w_uv[li], mla_w_o[li],
                                        cos, sin)
        x = modulated_sublayer(x, mod[:, 3:6], g[2], g[3], mixer, 1.0)
        x = modulated_sublayer(x, mod[:, 6:9], g[4], g[5],
                               lambda h: swiglu(h, ffn_w_in[i, 1], ffn_w_out[i, 1]), 0.5)
    return x
```

```python
import functools

import jax
import jax.numpy as jnp
from jax import lax
from jax.experimental import pallas as pl
from jax.experimental.pallas import tpu as pltpu

F32 = jnp.float32
BF16 = jnp.bfloat16

D_MODEL = 1024
DEPTH = 2
N_POOL_GROUPS = 4
POOL_GROUP = D_MODEL // N_POOL_GROUPS
POOL_WINDOWS = (2, 4, 8, 16)
N_HEADS = 16
QK_NOPE = 64
QK_ROPE = 32
V_HEAD = 64
Q_LORA = D_MODEL // 4
KV_LORA = D_MODEL // 8
ROPE_THETA = 10000.0
D_FF = 11 * D_MODEL // 4
EPS = 1e-6
N_MOD = 9
ATTN_SCALE = (QK_NOPE + QK_ROPE) ** -0.5

LANES = 128
SUBLANES = 8
MXU_DIM = 256
VMEM_LIMIT_BYTES = 56 * 1024 * 1024

FFN_CHUNK = MXU_DIM
N_FFN_CHUNKS = D_FF // FFN_CHUNK
FFN_ROWS = 512
POOL_ROWS = 512
POOL_HALO = SUBLANES
MLA_PRE_ROWS = 512
ATTN_ROWS = 256
QK_WIDTH = 2 * LANES
HALF_ROPE = QK_ROPE // 2


def _params(n_axes):
    return pltpu.CompilerParams(
        dimension_semantics=("arbitrary",) * n_axes,
        vmem_limit_bytes=VMEM_LIMIT_BYTES,
    )


def _resident(block_shape, index_map):
    return pl.BlockSpec(block_shape, index_map, pipeline_mode=pl.Buffered(1))


def _rms(v):
    return v * lax.rsqrt(jnp.mean(v * v, axis=-1, keepdims=True) + EPS)


def _modulated_norm(x, mod, g_pre):
    return _rms(x) * (g_pre * (1.0 + mod[1:2])) + mod[0:1]


def _residual(x, y, mod, g_post, weight):
    return x + (weight * (1.0 + mod[2:3])) * (_rms(y) * g_post)


ADA_COLS = 1152


def _ada_kernel(c_ref, w_ref, b_ref, o_ref):
    c = c_ref[...]
    sc = c * (1.0 / (1.0 + jnp.exp(-c)))
    o_ref[...] = jnp.dot(sc, w_ref[...], preferred_element_type=F32,
                         precision=lax.Precision.HIGHEST) + b_ref[...]


def _ada_mod(c, ada_w, ada_b):
    batch = c.shape[0]
    n_out = N_MOD * D_MODEL
    return pl.pallas_call(
        _ada_kernel,
        grid=(DEPTH, n_out // ADA_COLS),
        in_specs=[
            pl.BlockSpec((batch, D_MODEL), lambda l, j: (0, 0)),
            pl.BlockSpec((None, D_MODEL, ADA_COLS), lambda l, j: (l, 0, j)),
            pl.BlockSpec((None, 1, ADA_COLS), lambda l, j: (l, 0, j)),
        ],
        out_specs=pl.BlockSpec((None, batch, ADA_COLS), lambda l, j: (l, 0, j)),
        out_shape=jax.ShapeDtypeStruct((DEPTH, batch, n_out), F32),
        compiler_params=_params(2),
        name="ada_mod",
    )(c, ada_w, ada_b.reshape(DEPTH, 1, n_out))


def _ffn_kernel(x_ref, mod_ref, g_ref, win_ref, wout_ref, o_ref, act_ref):
    x = x_ref[...]
    mod = mod_ref[...]
    g = g_ref[...]
    hb = _modulated_norm(x, mod, g[0:1]).astype(BF16)
    for c in range(N_FFN_CHUNKS):
        gu = jnp.dot(hb, win_ref[c], preferred_element_type=F32)
        gate = gu[:, :FFN_CHUNK]
        up = gu[:, FFN_CHUNK:]
        act = gate * (1.0 / (1.0 + jnp.exp(-gate))) * up
        act_ref[:, c * FFN_CHUNK:(c + 1) * FFN_CHUNK] = act.astype(BF16)
    y = jnp.dot(act_ref[...], wout_ref[...], preferred_element_type=F32)
    o_ref[...] = _residual(x, y, mod, g[1:2], 0.5)


def _ffn_sublayer(x, mod, sub, g_pair, w_in_c, w_out_b):
    batch, seq, _ = x.shape
    return pl.pallas_call(
        _ffn_kernel,
        grid=(batch, seq // FFN_ROWS),
        in_specs=[
            pl.BlockSpec((None, FFN_ROWS, D_MODEL), lambda b, i: (b, i, 0)),
            pl.BlockSpec((None, None, 3, D_MODEL), lambda b, i: (b, sub, 0, 0)),
            pl.BlockSpec((2, D_MODEL), lambda b, i: (0, 0)),
            _resident((N_FFN_CHUNKS, D_MODEL, 2 * FFN_CHUNK), lambda b, i: (0, 0, 0)),
            _resident((D_FF, D_MODEL), lambda b, i: (0, 0)),
        ],
        out_specs=pl.BlockSpec((None, FFN_ROWS, D_MODEL), lambda b, i: (b, i, 0)),
        out_shape=jax.ShapeDtypeStruct(x.shape, F32),
        scratch_shapes=[pltpu.VMEM((FFN_ROWS, D_FF), BF16)],
        compiler_params=_params(2),
        name="ffn_sublayer",
    )(x, mod, g_pair, w_in_c, w_out_b)


def _ffn_weights(w_in, w_out):
    gate = w_in[:, :D_FF].reshape(D_MODEL, N_FFN_CHUNKS, FFN_CHUNK)
    up = w_in[:, D_FF:].reshape(D_MODEL, N_FFN_CHUNKS, FFN_CHUNK)
    w_in_c = jnp.concatenate([gate, up], axis=-1).transpose(1, 0, 2).astype(BF16)
    return w_in_c, w_out.astype(BF16)


def _pool_kernel(x_ref, xp_ref, xn_ref, mod_ref, g_ref, w_ref, b_ref, s_ref, o_ref, h_ref,
                 *, seq):
    i = pl.program_id(1)
    n_i = pl.num_programs(1)
    x = x_ref[...]
    mod = mod_ref[...]
    g = g_ref[...]
    rows = x.shape[0]
    h = _modulated_norm(x, mod, g[0:1])
    hp = _modulated_norm(xp_ref[...], mod, g[0:1]) * (i > 0).astype(F32)
    hn = _modulated_norm(xn_ref[...], mod, g[0:1]) * (i < n_i - 1).astype(F32)
    h_ref[0:POOL_HALO, :] = hp
    h_ref[POOL_HALO:POOL_HALO + rows, :] = h
    h_ref[POOL_HALO + rows:, :] = hn

    t = i * rows + lax.broadcasted_iota(jnp.int32, (rows, 1), 0)
    ys = []
    for gi, window in enumerate(POOL_WINDOWS):
        half = window // 2
        lanes = slice(gi * POOL_GROUP, (gi + 1) * POOL_GROUP)
        tot = h_ref[POOL_HALO - half:POOL_HALO - half + rows, lanes]
        for d in range(-half + 1, half):
            tot = tot + h_ref[POOL_HALO + d:POOL_HALO + d + rows, lanes]
        cnt = jnp.minimum(t + half, seq) - jnp.maximum(t - half, 0)
        pooled = tot / cnt.astype(F32)
        diff = (pooled - h[:, lanes]).astype(BF16)
        ys.append(jnp.dot(diff, w_ref[gi], preferred_element_type=F32))
    y = (jnp.concatenate(ys, axis=-1) + b_ref[...]) * s_ref[...]
    o_ref[...] = _residual(x, y, mod, g[1:2], 1.0)


def _pool_sublayer(x, mod, sub, g_pair, pool_w, pool_b, pool_scale):
    batch, seq, _ = x.shape
    halo_blocks = POOL_ROWS // POOL_HALO
    n_halo = seq // POOL_HALO
    return pl.pallas_call(
        functools.partial(_pool_kernel, seq=seq),
        grid=(batch, seq // POOL_ROWS),
        in_specs=[
            pl.BlockSpec((None, POOL_ROWS, D_MODEL), lambda b, i: (b, i, 0)),
            pl.BlockSpec((None, POOL_HALO, D_MODEL),
                         lambda b, i: (b, jnp.maximum(i * halo_blocks - 1, 0), 0)),
            pl.BlockSpec((None, POOL_HALO, D_MODEL),
                         lambda b, i: (b, jnp.minimum((i + 1) * halo_blocks, n_halo - 1), 0)),
            pl.BlockSpec((None, None, 3, D_MODEL), lambda b, i: (b, sub, 0, 0)),
            pl.BlockSpec((2, D_MODEL), lambda b, i: (0, 0)),
            pl.BlockSpec((N_POOL_GROUPS, POOL_GROUP, POOL_GROUP), lambda b, i: (0, 0, 0)),
            pl.BlockSpec((1, D_MODEL), lambda b, i: (0, 0)),
            pl.BlockSpec((1, D_MODEL), lambda b, i: (0, 0)),
        ],
        out_specs=pl.BlockSpec((None, POOL_ROWS, D_MODEL), lambda b, i: (b, i, 0)),
        out_shape=jax.ShapeDtypeStruct(x.shape, F32),
        scratch_shapes=[pltpu.VMEM((POOL_ROWS + 2 * POOL_HALO, D_MODEL), F32)],
        compiler_params=_params(2),
        name="pool_sublayer",
    )(x, x, x, mod, g_pair, pool_w.astype(BF16), pool_b.reshape(1, D_MODEL),
      pool_scale.reshape(1, D_MODEL))


LAT_COLS = Q_LORA + KV_LORA + 2 * LANES


def _fold_qk_kernel(wuq_ref, wuk_ref, o_ref):
    o_ref[...] = ATTN_SCALE * lax.dot_general(
        wuq_ref[:, :QK_NOPE], wuk_ref[...], (((1,), (1,)), ((), ())),
        preferred_element_type=F32, precision=lax.Precision.HIGHEST)


def _fold_ov_kernel(wuv_ref, wo_ref, o_ref):
    o_ref[...] = jnp.dot(wuv_ref[...], wo_ref[...], preferred_element_type=F32,
                         precision=lax.Precision.HIGHEST).astype(BF16)


def _mla_weights(w_in, w_uq, w_uk, w_uv, w_o):
    wuq_h = w_uq.transpose(1, 0, 2)
    wuk_h = w_uk.transpose(1, 0, 2)
    w_ql = pl.pallas_call(
        _fold_qk_kernel,
        grid=(N_HEADS,),
        in_specs=[
            pl.BlockSpec((None, Q_LORA, QK_NOPE + QK_ROPE), lambda h: (h, 0, 0)),
            pl.BlockSpec((None, KV_LORA, QK_NOPE), lambda h: (h, 0, 0)),
        ],
        out_specs=pl.BlockSpec((None, Q_LORA, KV_LORA), lambda h: (h, 0, 0)),
        out_shape=jax.ShapeDtypeStruct((N_HEADS, Q_LORA, KV_LORA), F32),
        compiler_params=_params(1),
        name="fold_qk",
    )(wuq_h, wuk_h)
    w_ov = pl.pallas_call(
        _fold_ov_kernel,
        grid=(N_HEADS,),
        in_specs=[
            pl.BlockSpec((None, KV_LORA, V_HEAD), lambda h: (h, 0, 0)),
            pl.BlockSpec((None, V_HEAD, D_MODEL), lambda h: (h, 0, 0)),
        ],
        out_specs=pl.BlockSpec((None, KV_LORA, D_MODEL), lambda h: (h, 0, 0)),
        out_shape=jax.ShapeDtypeStruct((N_HEADS, KV_LORA, D_MODEL), BF16),
        compiler_params=_params(1),
        name="fold_ov",
    )(w_uv.transpose(1, 0, 2), w_o.reshape(N_HEADS, V_HEAD, D_MODEL))

    def rope_cols(w, swap):
        a, b = w[..., :HALF_ROPE], w[..., HALF_ROPE:]
        first, second = (b, a) if swap else (a, b)
        pad = jnp.zeros(w.shape[:-1] + (LANES - QK_ROPE,), w.dtype)
        return jnp.concatenate([first, second, pad], axis=-1)

    wq_rope = wuq_h[..., QK_NOPE:]
    wq_main = jnp.concatenate([w_ql, rope_cols(wq_rope, False)], axis=-1).astype(BF16)
    wq_swap = rope_cols(wq_rope, True)
    wq_swap = wq_swap.reshape(N_HEADS // 2, 2, Q_LORA, LANES).transpose(0, 2, 1, 3)
    wq_swap = wq_swap.reshape(N_HEADS // 2, Q_LORA, 2 * LANES).astype(BF16)
    wk_rope = w_in[:, Q_LORA + KV_LORA:]
    w_lat = jnp.concatenate(
        [w_in[:, :Q_LORA + KV_LORA], rope_cols(wk_rope, False), rope_cols(wk_rope, True)],
        axis=-1).astype(BF16)
    return w_lat, wq_main, wq_swap, w_ov


def _rope_tables(seq):
    inv = 1.0 / (ROPE_THETA ** (jnp.arange(0, QK_ROPE, 2, dtype=F32) / QK_ROPE))
    ang = jnp.arange(seq, dtype=F32)[:, None] * inv[None, :]
    cos, sin = jnp.cos(ang), jnp.sin(ang)
    pad = jnp.zeros((seq, LANES - QK_ROPE), F32)
    return (jnp.concatenate([cos, cos, pad], axis=-1),
            jnp.concatenate([-sin, sin, pad], axis=-1))


def _mla_pre_kernel(x_ref, mod_ref, g_ref, wlat_ref, qn_ref, kvn_ref, wq_ref, wqs_ref,
                    cos_ref, sin_ref, q_ref, k_ref):
    x = x_ref[...]
    hb = _modulated_norm(x, mod_ref[...], g_ref[0:1, :]).astype(BF16)
    lat = jnp.dot(hb, wlat_ref[...], preferred_element_type=F32)
    c_q = _rms(lat[:, :Q_LORA]) * qn_ref[...]
    c_kv = _rms(lat[:, Q_LORA:Q_LORA + KV_LORA]) * kvn_ref[...]
    cos = cos_ref[...]
    sin = sin_ref[...]
    k_rope = (lat[:, Q_LORA + KV_LORA:Q_LORA + KV_LORA + LANES] * cos
              + lat[:, Q_LORA + KV_LORA + LANES:] * sin)
    k_ref[...] = jnp.concatenate([c_kv, k_rope], axis=-1).astype(BF16)
    cqb = c_q.astype(BF16)
    cos_q = cos * ATTN_SCALE
    sin_q = sin * ATTN_SCALE
    for pair in range(N_HEADS // 2):
        swapped = jnp.dot(cqb, wqs_ref[pair], preferred_element_type=F32)
        for j in range(2):
            head = 2 * pair + j
            a = jnp.dot(cqb, wq_ref[head], preferred_element_type=F32)
            rope = a[:, LANES:] * cos_q + swapped[:, j * LANES:(j + 1) * LANES] * sin_q
            q_ref[head] = jnp.concatenate([a[:, :LANES], rope], axis=-1).astype(BF16)


def _mla_pre(x, mod, sub, g_pair, w_lat, q_norm, kv_norm, wq_main, wq_swap, cos_t, sin_t):
    batch, seq, _ = x.shape
    rows = MLA_PRE_ROWS
    return pl.pallas_call(
        _mla_pre_kernel,
        grid=(batch, seq // rows),
        in_specs=[
            pl.BlockSpec((None, rows, D_MODEL), lambda b, i: (b, i, 0)),
            pl.BlockSpec((None, None, 3, D_MODEL), lambda b, i: (b, sub, 0, 0)),
            pl.BlockSpec((2, D_MODEL), lambda b, i: (0, 0)),
            pl.BlockSpec((D_MODEL, LAT_COLS), lambda b, i: (0, 0)),
            pl.BlockSpec((1, Q_LORA), lambda b, i: (0, 0)),
            pl.BlockSpec((1, KV_LORA), lambda b, i: (0, 0)),
            pl.BlockSpec((N_HEADS, Q_LORA, QK_WIDTH), lambda b, i: (0, 0, 0)),
            pl.BlockSpec((N_HEADS // 2, Q_LORA, 2 * LANES), lambda b, i: (0, 0, 0)),
            pl.BlockSpec((rows, LANES), lambda b, i: (i, 0)),
            pl.BlockSpec((rows, LANES), lambda b, i: (i, 0)),
        ],
        out_specs=[
            pl.BlockSpec((None, N_HEADS, rows, QK_WIDTH), lambda b, i: (b, 0, i, 0)),
            pl.BlockSpec((None, rows, QK_WIDTH), lambda b, i: (b, i, 0)),
        ],
        out_shape=[
            jax.ShapeDtypeStruct((batch, N_HEADS, seq, QK_WIDTH), BF16),
            jax.ShapeDtypeStruct((batch, seq, QK_WIDTH), BF16),
        ],
        compiler_params=_params(2),
        name="mla_pre",
    )(x, mod, g_pair, w_lat, q_norm.reshape(1, Q_LORA), kv_norm.reshape(1, KV_LORA),
      wq_main, wq_swap, cos_t, sin_t)


def _mla_attn_kernel(q_ref, k_ref, x_ref, mod_ref, g_ref, wov_ref, o_ref, acc_ref):
    acc_ref[...] = jnp.zeros_like(acc_ref)

    def head_body(head, carry):
        q = q_ref[head]
        s = lax.dot_general(q, k_ref[...], (((1,), (1,)), ((), ())),
                            preferred_element_type=F32)
        m = jnp.max(s, axis=-1, keepdims=True)
        p = jnp.exp(s - m)
        l = jnp.sum(p, axis=-1, keepdims=True)
        o = jnp.dot(p.astype(BF16), k_ref[:, :KV_LORA], preferred_element_type=F32)
        o = o * (1.0 / l)
        acc_ref[...] += jnp.dot(o.astype(BF16), wov_ref[head], preferred_element_type=F32)
        return carry

    lax.fori_loop(0, N_HEADS, head_body, 0)
    o_ref[...] = _residual(x_ref[...], acc_ref[...], mod_ref[...], g_ref[1:2, :], 1.0)


def _mla_attn(x, q, k, mod, sub, g_pair, w_ov):
    batch, seq, _ = x.shape
    rows = ATTN_ROWS
    return pl.pallas_call(
        _mla_attn_kernel,
        grid=(batch, seq // rows),
        in_specs=[
            pl.BlockSpec((None, N_HEADS, rows, QK_WIDTH), lambda b, i: (b, 0, i, 0)),
            pl.BlockSpec((None, seq, QK_WIDTH), lambda b, i: (b, 0, 0)),
            pl.BlockSpec((None, rows, D_MODEL), lambda b, i: (b, i, 0)),
            pl.BlockSpec((None, None, 3, D_MODEL), lambda b, i: (b, sub, 0, 0)),
            pl.BlockSpec((2, D_MODEL), lambda b, i: (0, 0)),
            _resident((N_HEADS, KV_LORA, D_MODEL), lambda b, i: (0, 0, 0)),
        ],
        out_specs=pl.BlockSpec((None, rows, D_MODEL), lambda b, i: (b, i, 0)),
        out_shape=jax.ShapeDtypeStruct(x.shape, F32),
        scratch_shapes=[pltpu.VMEM((rows, D_MODEL), F32)],
        compiler_params=_params(2),
        name="mla_attn",
    )(q, k, x, mod, g_pair, w_ov)


def kernel(x, c, ada_w, ada_b, norm_g, ffn_w_in, ffn_w_out, pool_w, pool_b, pool_scale,
           mla_w_in, mla_q_norm, mla_kv_norm, mla_w_uq, mla_w_uk, mla_w_uv, mla_w_o):
    batch, seq, _ = x.shape
    mods = _ada_mod(c, ada_w, ada_b).reshape(DEPTH, batch, 3, 3, D_MODEL)
    cos_t, sin_t = _rope_tables(seq)
    for i in range(DEPTH):
        mod = mods[i]
        g = norm_g[i]
        li = i // 2
        x = _ffn_sublayer(x, mod, 0, g[0:2], *_ffn_weights(ffn_w_in[i, 0], ffn_w_out[i, 0]))
        if i % 2 == 0:
            x = _pool_sublayer(x, mod, 1, g[2:4], pool_w[li], pool_b[li], pool_scale[li])
        else:
            w_lat, wq_main, wq_swap, w_ov = _mla_weights(
                mla_w_in[li], mla_w_uq[li], mla_w_uk[li], mla_w_uv[li], mla_w_o[li])
            q, k = _mla_pre(x, mod, 1, g[2:4], w_lat, mla_q_norm[li], mla_kv_norm[li],
                            wq_main, wq_swap, cos_t, sin_t)
            x = _mla_attn(x, q, k, mod, 1, g[2:4], w_ov)
        x = _ffn_sublayer(x, mod, 2, g[4:6], *_ffn_weights(ffn_w_in[i, 1], ffn_w_out[i, 1]))
    return x
```

```python
import functools

import jax
import jax.numpy as jnp
from jax import lax
from jax.experimental import pallas as pl
from jax.experimental.pallas import tpu as pltpu

F32 = jnp.float32
BF16 = jnp.bfloat16

D_MODEL = 1024
DEPTH = 2
N_POOL_GROUPS = 4
POOL_GROUP = D_MODEL // N_POOL_GROUPS
POOL_WINDOWS = (2, 4, 8, 16)
N_HEADS = 16
QK_NOPE = 64
QK_ROPE = 32
V_HEAD = 64
Q_LORA = D_MODEL // 4
KV_LORA = D_MODEL // 8
ROPE_THETA = 10000.0
D_FF = 11 * D_MODEL // 4
EPS = 1e-6
N_MOD = 9
ATTN_SCALE = (QK_NOPE + QK_ROPE) ** -0.5
LOG2_E = 1.4426950408889634
Q_SCALE = ATTN_SCALE * LOG2_E

LANES = 128
SUBLANES = 8
MXU_DIM = 256
VMEM_LIMIT_BYTES = 56 * 1024 * 1024

FFN_CHUNK = MXU_DIM
N_FFN_CHUNKS = D_FF // FFN_CHUNK
FFN_ROWS = 512
POOL_ROWS = 512
POOL_HALO = SUBLANES
MLA_PRE_ROWS = 512
ATTN_ROWS = 512
QK_WIDTH = 2 * LANES
HALF_ROPE = QK_ROPE // 2


def _params(n_axes):
    return pltpu.CompilerParams(
        dimension_semantics=("arbitrary",) * n_axes,
        vmem_limit_bytes=VMEM_LIMIT_BYTES,
    )


def _resident(block_shape, index_map):
    return pl.BlockSpec(block_shape, index_map, pipeline_mode=pl.Buffered(1))


def _rms(v):
    return v * lax.rsqrt(jnp.mean(v * v, axis=-1, keepdims=True) + EPS)


def _modulated_norm(x, mod, g_pre):
    return _rms(x) * (g_pre * (1.0 + mod[1:2])) + mod[0:1]


def _residual(x, y, mod, g_post, weight):
    return x + (weight * (1.0 + mod[2:3])) * (_rms(y) * g_post)


ADA_COLS = 1152


def _ada_kernel(c_ref, w_ref, b_ref, o_ref):
    c = c_ref[...]
    sc = c * (1.0 / (1.0 + jnp.exp(-c)))
    o_ref[...] = jnp.dot(sc, w_ref[...], preferred_element_type=F32,
                         precision=lax.Precision.HIGHEST) + b_ref[...]


def _ada_mod(c, ada_w, ada_b):
    batch = c.shape[0]
    n_out = N_MOD * D_MODEL
    return pl.pallas_call(
        _ada_kernel,
        grid=(DEPTH, n_out // ADA_COLS),
        in_specs=[
            pl.BlockSpec((batch, D_MODEL), lambda l, j: (0, 0)),
            pl.BlockSpec((None, D_MODEL, ADA_COLS), lambda l, j: (l, 0, j)),
            pl.BlockSpec((None, 1, ADA_COLS), lambda l, j: (l, 0, j)),
        ],
        out_specs=pl.BlockSpec((None, batch, ADA_COLS), lambda l, j: (l, 0, j)),
        out_shape=jax.ShapeDtypeStruct((DEPTH, batch, n_out), F32),
        compiler_params=_params(2),
        name="ada_mod",
    )(c, ada_w, ada_b.reshape(DEPTH, 1, n_out))


def _ffn_kernel(x_ref, mod_ref, g_ref, win_ref, wout_ref, o_ref, act_ref):
    x = x_ref[...]
    mod = mod_ref[...]
    g = g_ref[...]
    hb = _modulated_norm(x, mod, g[0:1]).astype(BF16)
    for c in range(N_FFN_CHUNKS):
        gu = jnp.dot(hb, win_ref[c], preferred_element_type=F32)
        gate = gu[:, :FFN_CHUNK]
        up = gu[:, FFN_CHUNK:]
        act = gate * (1.0 / (1.0 + jnp.exp(-gate))) * up
        act_ref[:, c * FFN_CHUNK:(c + 1) * FFN_CHUNK] = act.astype(BF16)
    y = jnp.dot(act_ref[...], wout_ref[...], preferred_element_type=F32)
    o_ref[...] = _residual(x, y, mod, g[1:2], 0.5)


def _ffn_sublayer(x, mod, sub, g_pair, w_in_c, w_out_b):
    batch, seq, _ = x.shape
    return pl.pallas_call(
        _ffn_kernel,
        grid=(batch, seq // FFN_ROWS),
        in_specs=[
            pl.BlockSpec((None, FFN_ROWS, D_MODEL), lambda b, i: (b, i, 0)),
            pl.BlockSpec((None, None, 3, D_MODEL), lambda b, i: (b, sub, 0, 0)),
            pl.BlockSpec((2, D_MODEL), lambda b, i: (0, 0)),
            _resident((N_FFN_CHUNKS, D_MODEL, 2 * FFN_CHUNK), lambda b, i: (0, 0, 0)),
            _resident((D_FF, D_MODEL), lambda b, i: (0, 0)),
        ],
        out_specs=pl.BlockSpec((None, FFN_ROWS, D_MODEL), lambda b, i: (b, i, 0)),
        out_shape=jax.ShapeDtypeStruct(x.shape, F32),
        scratch_shapes=[pltpu.VMEM((FFN_ROWS, D_FF), BF16)],
        compiler_params=_params(2),
        name="ffn_sublayer",
    )(x, mod, g_pair, w_in_c, w_out_b)


def _ffn_weights(w_in, w_out):
    gate = w_in[:, :D_FF].reshape(D_MODEL, N_FFN_CHUNKS, FFN_CHUNK)
    up = w_in[:, D_FF:].reshape(D_MODEL, N_FFN_CHUNKS, FFN_CHUNK)
    w_in_c = jnp.concatenate([gate, up], axis=-1).transpose(1, 0, 2).astype(BF16)
    return w_in_c, w_out.astype(BF16)


def _pool_kernel(x_ref, xp_ref, xn_ref, mod_ref, g_ref, w_ref, b_ref, s_ref, o_ref, h_ref,
                 *, seq):
    i = pl.program_id(1)
    n_i = pl.num_programs(1)
    x = x_ref[...]
    mod = mod_ref[...]
    g = g_ref[...]
    rows = x.shape[0]
    h = _modulated_norm(x, mod, g[0:1])
    hp = _modulated_norm(xp_ref[...], mod, g[0:1]) * (i > 0).astype(F32)
    hn = _modulated_norm(xn_ref[...], mod, g[0:1]) * (i < n_i - 1).astype(F32)
    h_ref[0:POOL_HALO, :] = hp
    h_ref[POOL_HALO:POOL_HALO + rows, :] = h
    h_ref[POOL_HALO + rows:, :] = hn

    t = i * rows + lax.broadcasted_iota(jnp.int32, (rows, 1), 0)
    ys = []
    for gi, window in enumerate(POOL_WINDOWS):
        half = window // 2
        lanes = slice(gi * POOL_GROUP, (gi + 1) * POOL_GROUP)
        tot = h_ref[POOL_HALO - half:POOL_HALO - half + rows, lanes]
        for d in range(-half + 1, half):
            tot = tot + h_ref[POOL_HALO + d:POOL_HALO + d + rows, lanes]
        cnt = jnp.minimum(t + half, seq) - jnp.maximum(t - half, 0)
        pooled = tot / cnt.astype(F32)
        diff = (pooled - h[:, lanes]).astype(BF16)
        ys.append(jnp.dot(diff, w_ref[gi], preferred_element_type=F32))
    y = (jnp.concatenate(ys, axis=-1) + b_ref[...]) * s_ref[...]
    o_ref[...] = _residual(x, y, mod, g[1:2], 1.0)


def _pool_sublayer(x, mod, sub, g_pair, pool_w, pool_b, pool_scale):
    batch, seq, _ = x.shape
    halo_blocks = POOL_ROWS // POOL_HALO
    n_halo = seq // POOL_HALO
    return pl.pallas_call(
        functools.partial(_pool_kernel, seq=seq),
        grid=(batch, seq // POOL_ROWS),
        in_specs=[
            pl.BlockSpec((None, POOL_ROWS, D_MODEL), lambda b, i: (b, i, 0)),
            pl.BlockSpec((None, POOL_HALO, D_MODEL),
                         lambda b, i: (b, jnp.maximum(i * halo_blocks - 1, 0), 0)),
            pl.BlockSpec((None, POOL_HALO, D_MODEL),
                         lambda b, i: (b, jnp.minimum((i + 1) * halo_blocks, n_halo - 1), 0)),
            pl.BlockSpec((None, None, 3, D_MODEL), lambda b, i: (b, sub, 0, 0)),
            pl.BlockSpec((2, D_MODEL), lambda b, i: (0, 0)),
            pl.BlockSpec((N_POOL_GROUPS, POOL_GROUP, POOL_GROUP), lambda b, i: (0, 0, 0)),
            pl.BlockSpec((1, D_MODEL), lambda b, i: (0, 0)),
            pl.BlockSpec((1, D_MODEL), lambda b, i: (0, 0)),
        ],
        out_specs=pl.BlockSpec((None, POOL_ROWS, D_MODEL), lambda b, i: (b, i, 0)),
        out_shape=jax.ShapeDtypeStruct(x.shape, F32),
        scratch_shapes=[pltpu.VMEM((POOL_ROWS + 2 * POOL_HALO, D_MODEL), F32)],
        compiler_params=_params(2),
        name="pool_sublayer",
    )(x, x, x, mod, g_pair, pool_w.astype(BF16), pool_b.reshape(1, D_MODEL),
      pool_scale.reshape(1, D_MODEL))


LAT_COLS = Q_LORA + KV_LORA + 2 * LANES


def _fold_qk_kernel(wuq_ref, wuk_ref, o_ref):
    o_ref[...] = Q_SCALE * lax.dot_general(
        wuq_ref[:, :QK_NOPE], wuk_ref[...], (((1,), (1,)), ((), ())),
        preferred_element_type=F32, precision=lax.Precision.HIGHEST)


def _mla_weights(w_in, w_uq, w_uk, w_uv, w_o):
    wuq_h = w_uq.transpose(1, 0, 2)
    wuk_h = w_uk.transpose(1, 0, 2)
    w_ql = pl.pallas_call(
        _fold_qk_kernel,
        grid=(N_HEADS,),
        in_specs=[
            pl.BlockSpec((None, Q_LORA, QK_NOPE + QK_ROPE), lambda h: (h, 0, 0)),
            pl.BlockSpec((None, KV_LORA, QK_NOPE), lambda h: (h, 0, 0)),
        ],
        out_specs=pl.BlockSpec((None, Q_LORA, KV_LORA), lambda h: (h, 0, 0)),
        out_shape=jax.ShapeDtypeStruct((N_HEADS, Q_LORA, KV_LORA), F32),
        compiler_params=_params(1),
        name="fold_qk",
    )(wuq_h, wuk_h)

    def rope_cols(w, swap):
        a, b = w[..., :HALF_ROPE], w[..., HALF_ROPE:]
        first, second = (b, a) if swap else (a, b)
        pad = jnp.zeros(w.shape[:-1] + (LANES - QK_ROPE,), w.dtype)
        return jnp.concatenate([first, second, pad], axis=-1)

    wq_rope = wuq_h[..., QK_NOPE:]
    wq_main = jnp.concatenate([w_ql, rope_cols(wq_rope, False)], axis=-1).astype(BF16)
    wq_swap = rope_cols(wq_rope, True)
    wq_swap = wq_swap.reshape(N_HEADS // 2, 2, Q_LORA, LANES).transpose(0, 2, 1, 3)
    wq_swap = wq_swap.reshape(N_HEADS // 2, Q_LORA, 2 * LANES).astype(BF16)
    wk_rope = w_in[:, Q_LORA + KV_LORA:]
    w_lat = jnp.concatenate(
        [w_in[:, :Q_LORA + KV_LORA], rope_cols(wk_rope, False), rope_cols(wk_rope, True)],
        axis=-1).astype(BF16)
    w_uvt = w_uv.transpose(1, 2, 0).astype(BF16)
    return w_lat, wq_main, wq_swap, w_uvt, w_o.T.astype(BF16)


def _rope_tables(seq):
    inv = 1.0 / (ROPE_THETA ** (jnp.arange(0, QK_ROPE, 2, dtype=F32) / QK_ROPE))
    ang = jnp.arange(seq, dtype=F32)[:, None] * inv[None, :]
    cos, sin = jnp.cos(ang), jnp.sin(ang)
    pad = jnp.zeros((seq, LANES - QK_ROPE), F32)
    return (jnp.concatenate([cos, cos, pad], axis=-1),
            jnp.concatenate([-sin, sin, pad], axis=-1))


def _mla_pre_kernel(x_ref, mod_ref, g_ref, wlat_ref, qn_ref, kvn_ref, wq_ref, wqs_ref,
                    cos_ref, sin_ref, q_ref, k_ref, vt_ref):
    x = x_ref[...]
    hb = _modulated_norm(x, mod_ref[...], g_ref[0:1, :]).astype(BF16)
    lat = jnp.dot(hb, wlat_ref[...], preferred_element_type=F32)
    c_q = _rms(lat[:, :Q_LORA]) * qn_ref[...]
    c_kv = _rms(lat[:, Q_LORA:Q_LORA + KV_LORA]) * kvn_ref[...]
    cos = cos_ref[...]
    sin = sin_ref[...]
    k_rope = (lat[:, Q_LORA + KV_LORA:Q_LORA + KV_LORA + LANES] * cos
              + lat[:, Q_LORA + KV_LORA + LANES:] * sin)
    k_ref[...] = jnp.concatenate([c_kv, k_rope], axis=-1).astype(BF16)
    vt_ref[...] = c_kv.T.astype(BF16)
    cqb = c_q.astype(BF16)
    cos_q = cos * Q_SCALE
    sin_q = sin * Q_SCALE
    for pair in range(N_HEADS // 2):
        swapped = jnp.dot(cqb, wqs_ref[pair], preferred_element_type=F32)
        for j in range(2):
            head = 2 * pair + j
            a = jnp.dot(cqb, wq_ref[head], preferred_element_type=F32)
            rope = a[:, LANES:] * cos_q + swapped[:, j * LANES:(j + 1) * LANES] * sin_q
            q_ref[head] = jnp.concatenate([a[:, :LANES], rope], axis=-1).astype(BF16)


def _mla_pre(x, mod, sub, g_pair, w_lat, q_norm, kv_norm, wq_main, wq_swap, cos_t, sin_t):
    batch, seq, _ = x.shape
    rows = MLA_PRE_ROWS
    return pl.pallas_call(
        _mla_pre_kernel,
        grid=(batch, seq // rows),
        in_specs=[
            pl.BlockSpec((None, rows, D_MODEL), lambda b, i: (b, i, 0)),
            pl.BlockSpec((None, None, 3, D_MODEL), lambda b, i: (b, sub, 0, 0)),
            pl.BlockSpec((2, D_MODEL), lambda b, i: (0, 0)),
            pl.BlockSpec((D_MODEL, LAT_COLS), lambda b, i: (0, 0)),
            pl.BlockSpec((1, Q_LORA), lambda b, i: (0, 0)),
            pl.BlockSpec((1, KV_LORA), lambda b, i: (0, 0)),
            pl.BlockSpec((N_HEADS, Q_LORA, QK_WIDTH), lambda b, i: (0, 0, 0)),
            pl.BlockSpec((N_HEADS // 2, Q_LORA, 2 * LANES), lambda b, i: (0, 0, 0)),
            pl.BlockSpec((rows, LANES), lambda b, i: (i, 0)),
            pl.BlockSpec((rows, LANES), lambda b, i: (i, 0)),
        ],
        out_specs=[
            pl.BlockSpec((None, N_HEADS, rows, QK_WIDTH), lambda b, i: (b, 0, i, 0)),
            pl.BlockSpec((None, rows, QK_WIDTH), lambda b, i: (b, i, 0)),
            pl.BlockSpec((None, KV_LORA, rows), lambda b, i: (b, 0, i)),
        ],
        out_shape=[
            jax.ShapeDtypeStruct((batch, N_HEADS, seq, QK_WIDTH), BF16),
            jax.ShapeDtypeStruct((batch, seq, QK_WIDTH), BF16),
            jax.ShapeDtypeStruct((batch, KV_LORA, seq), BF16),
        ],
        compiler_params=_params(2),
        name="mla_pre",
    )(x, mod, g_pair, w_lat, q_norm.reshape(1, Q_LORA), kv_norm.reshape(1, KV_LORA),
      wq_main, wq_swap, cos_t, sin_t)


def _mla_attn_kernel(q_ref, k_ref, vt_ref, x_ref, mod_ref, g_ref, wuvt_ref, wot_ref, o_ref,
                     st_ref, m_ref, ov_ref):
    def scores(head, slot):
        st = lax.dot_general(k_ref[...], q_ref[head], (((1,), (1,)), ((), ())),
                             preferred_element_type=F32)
        st_ref[slot] = st
        m_ref[slot] = jnp.max(st, axis=0, keepdims=True)

    def finish(head, slot):
        pt = jnp.exp2(st_ref[slot] - m_ref[slot])
        l = jnp.sum(pt, axis=0, keepdims=True)
        ot = jnp.dot(vt_ref[...], pt.astype(BF16), preferred_element_type=F32)
        ot = ot * (1.0 / l)
        ovt = jnp.dot(wuvt_ref[head], ot.astype(BF16), preferred_element_type=F32)
        ov_ref[pl.ds(pl.multiple_of(head * V_HEAD, V_HEAD), V_HEAD), :] = ovt.astype(BF16)

    def pair_body(j, carry):
        scores(2 * j + 1, 1)
        finish(2 * j, 0)
        scores(2 * j + 2, 0)
        finish(2 * j + 1, 1)
        return carry

    scores(0, 0)
    lax.fori_loop(0, N_HEADS // 2 - 1, pair_body, 0)
    scores(N_HEADS - 1, 1)
    finish(N_HEADS - 2, 0)
    finish(N_HEADS - 1, 1)
    yt = jnp.dot(wot_ref[...], ov_ref[...], preferred_element_type=F32)
    o_ref[...] = _residual(x_ref[...], yt.T, mod_ref[...], g_ref[1:2, :], 1.0)


def _mla_attn(x, q, k, vt, mod, sub, g_pair, w_uvt, w_ot):
    batch, seq, _ = x.shape
    rows = ATTN_ROWS
    return pl.pallas_call(
        _mla_attn_kernel,
        grid=(batch, seq // rows),
        in_specs=[
            pl.BlockSpec((None, N_HEADS, rows, QK_WIDTH), lambda b, i: (b, 0, i, 0)),
            pl.BlockSpec((None, seq, QK_WIDTH), lambda b, i: (b, 0, 0)),
            pl.BlockSpec((None, KV_LORA, seq), lambda b, i: (b, 0, 0)),
            pl.BlockSpec((None, rows, D_MODEL), lambda b, i: (b, i, 0)),
            pl.BlockSpec((None, None, 3, D_MODEL), lambda b, i: (b, sub, 0, 0)),
            pl.BlockSpec((2, D_MODEL), lambda b, i: (0, 0)),
            _resident((N_HEADS, V_HEAD, KV_LORA), lambda b, i: (0, 0, 0)),
            _resident((D_MODEL, N_HEADS * V_HEAD), lambda b, i: (0, 0)),
        ],
        out_specs=pl.BlockSpec((None, rows, D_MODEL), lambda b, i: (b, i, 0)),
        out_shape=jax.ShapeDtypeStruct(x.shape, F32),
        scratch_shapes=[
            pltpu.VMEM((2, seq, rows), F32),
            pltpu.VMEM((2, 1, rows), F32),
            pltpu.VMEM((N_HEADS * V_HEAD, rows), BF16),
        ],
        compiler_params=_params(2),
        name="mla_attn",
    )(q, k, vt, x, mod, g_pair, w_uvt, w_ot)


def kernel(x, c, ada_w, ada_b, norm_g, ffn_w_in, ffn_w_out, pool_w, pool_b, pool_scale,
           mla_w_in, mla_q_norm, mla_kv_norm, mla_w_uq, mla_w_uk, mla_w_uv, mla_w_o):
    batch, seq, _ = x.shape
    mods = _ada_mod(c, ada_w, ada_b).reshape(DEPTH, batch, 3, 3, D_MODEL)
    cos_t, sin_t = _rope_tables(seq)
    for i in range(DEPTH):
        mod = mods[i]
        g = norm_g[i]
        li = i // 2
        x = _ffn_sublayer(x, mod, 0, g[0:2], *_ffn_weights(ffn_w_in[i, 0], ffn_w_out[i, 0]))
        if i % 2 == 0:
            x = _pool_sublayer(x, mod, 1, g[2:4], pool_w[li], pool_b[li], pool_scale[li])
        else:
            w_lat, wq_main, wq_swap, w_uvt, w_ot = _mla_weights(
                mla_w_in[li], mla_w_uq[li], mla_w_uk[li], mla_w_uv[li], mla_w_o[li])
            q, k, vt = _mla_pre(x, mod, 1, g[2:4], w_lat, mla_q_norm[li], mla_kv_norm[li],
                                wq_main, wq_swap, cos_t, sin_t)
            x = _mla_attn(x, q, k, vt, mod, 1, g[2:4], w_uvt, w_ot)
        x = _ffn_sublayer(x, mod, 2, g[4:6], *_ffn_weights(ffn_w_in[i, 1], ffn_w_out[i, 1]))
    return x
```

```python
import functools

import jax
import jax.numpy as jnp
import numpy as np
from jax import lax
from jax.experimental import pallas as pl
from jax.experimental.pallas import tpu as pltpu

F32 = jnp.float32
BF16 = jnp.bfloat16

D_MODEL = 1024
DEPTH = 2
N_POOL_GROUPS = 4
POOL_GROUP = D_MODEL // N_POOL_GROUPS
POOL_WINDOWS = (2, 4, 8, 16)
N_HEADS = 16
QK_NOPE = 64
QK_ROPE = 32
V_HEAD = 64
Q_LORA = D_MODEL // 4
KV_LORA = D_MODEL // 8
ROPE_THETA = 10000.0
D_FF = 11 * D_MODEL // 4
EPS = 1e-6
N_MOD = 9
ATTN_SCALE = (QK_NOPE + QK_ROPE) ** -0.5
LOG2_E = 1.4426950408889634
Q_SCALE = ATTN_SCALE * LOG2_E

LANES = 128
SUBLANES = 8
MXU_DIM = 256
VMEM_LIMIT_BYTES = 56 * 1024 * 1024

FFN_CHUNK = MXU_DIM
N_FFN_CHUNKS = D_FF // FFN_CHUNK
FFN_ROWS = 512
POOL_ROWS = 512
POOL_HALO = SUBLANES
MLA_PRE_ROWS = 512
ATTN_ROWS = 512
KEY_TILE = MXU_DIM
SCORE_LEAD = 2
QK_WIDTH = 2 * LANES
HALF_ROPE = QK_ROPE // 2
QK_FEATS = KV_LORA + QK_ROPE
BF16_SUBLANES = 2 * SUBLANES
VT_ROWS = KV_LORA + BF16_SUBLANES


def _params(n_axes, flags=None):
    return pltpu.CompilerParams(
        dimension_semantics=("arbitrary",) * n_axes,
        vmem_limit_bytes=VMEM_LIMIT_BYTES,
        flags=flags,
    )


def _resident(block_shape, index_map):
    return pl.BlockSpec(block_shape, index_map, pipeline_mode=pl.Buffered(1))


def _rms(v):
    return v * lax.rsqrt(jnp.mean(v * v, axis=-1, keepdims=True) + EPS)


def _modulated_norm(x, mod, g_pre):
    return _rms(x) * (g_pre * (1.0 + mod[1:2])) + mod[0:1]


def _residual(x, y, mod, g_post, weight):
    return x + (weight * (1.0 + mod[2:3])) * (_rms(y) * g_post)


ADA_COLS = 1152


def _ada_kernel(c_ref, w_ref, b_ref, o_ref):
    c = c_ref[...]
    sc = c * (1.0 / (1.0 + jnp.exp(-c)))
    o_ref[...] = jnp.dot(sc, w_ref[...], preferred_element_type=F32,
                         precision=lax.Precision.HIGHEST) + b_ref[...]


def _ada_mod(c, ada_w, ada_b):
    batch = c.shape[0]
    n_out = N_MOD * D_MODEL
    return pl.pallas_call(
        _ada_kernel,
        grid=(DEPTH, n_out // ADA_COLS),
        in_specs=[
            pl.BlockSpec((batch, D_MODEL), lambda l, j: (0, 0)),
            pl.BlockSpec((None, D_MODEL, ADA_COLS), lambda l, j: (l, 0, j)),
            pl.BlockSpec((None, 1, ADA_COLS), lambda l, j: (l, 0, j)),
        ],
        out_specs=pl.BlockSpec((None, batch, ADA_COLS), lambda l, j: (l, 0, j)),
        out_shape=jax.ShapeDtypeStruct((DEPTH, batch, n_out), F32),
        compiler_params=_params(2),
        name="ada_mod",
    )(c, ada_w, ada_b.reshape(DEPTH, 1, n_out))


def _ffn_kernel(x_ref, mod_ref, g_ref, win_ref, wout_ref, o_ref, act_ref):
    x = x_ref[...]
    mod = mod_ref[...]
    g = g_ref[...]
    hb = _modulated_norm(x, mod, g[0:1]).astype(BF16)
    for c in range(N_FFN_CHUNKS):
        gu = jnp.dot(hb, win_ref[c], preferred_element_type=F32)
        gate = gu[:, :FFN_CHUNK]
        up = gu[:, FFN_CHUNK:]
        act = gate * (1.0 / (1.0 + jnp.exp(-gate))) * up
        act_ref[:, c * FFN_CHUNK:(c + 1) * FFN_CHUNK] = act.astype(BF16)
    y = jnp.dot(act_ref[...], wout_ref[...], preferred_element_type=F32)
    o_ref[...] = _residual(x, y, mod, g[1:2], 0.5)


def _ffn_sublayer(x, mod, sub, g_pair, w_in_c, w_out_b):
    batch, seq, _ = x.shape
    return pl.pallas_call(
        _ffn_kernel,
        grid=(batch, seq // FFN_ROWS),
        in_specs=[
            pl.BlockSpec((None, FFN_ROWS, D_MODEL), lambda b, i: (b, i, 0)),
            pl.BlockSpec((None, None, 3, D_MODEL), lambda b, i: (b, sub, 0, 0)),
            pl.BlockSpec((2, D_MODEL), lambda b, i: (0, 0)),
            _resident((N_FFN_CHUNKS, D_MODEL, 2 * FFN_CHUNK), lambda b, i: (0, 0, 0)),
            _resident((D_FF, D_MODEL), lambda b, i: (0, 0)),
        ],
        out_specs=pl.BlockSpec((None, FFN_ROWS, D_MODEL), lambda b, i: (b, i, 0)),
        out_shape=jax.ShapeDtypeStruct(x.shape, F32),
        scratch_shapes=[pltpu.VMEM((FFN_ROWS, D_FF), BF16)],
        compiler_params=_params(2),
        name="ffn_sublayer",
    )(x, mod, g_pair, w_in_c, w_out_b)


def _ffn_weights(w_in, w_out):
    gate = w_in[:, :D_FF].reshape(D_MODEL, N_FFN_CHUNKS, FFN_CHUNK)
    up = w_in[:, D_FF:].reshape(D_MODEL, N_FFN_CHUNKS, FFN_CHUNK)
    w_in_c = jnp.concatenate([gate, up], axis=-1).transpose(1, 0, 2).astype(BF16)
    return w_in_c, w_out.astype(BF16)


def _pool_kernel(x_ref, xp_ref, xn_ref, mod_ref, g_ref, w_ref, b_ref, s_ref, o_ref, h_ref,
                 *, seq):
    i = pl.program_id(1)
    n_i = pl.num_programs(1)
    x = x_ref[...]
    mod = mod_ref[...]
    g = g_ref[...]
    rows = x.shape[0]
    h = _modulated_norm(x, mod, g[0:1])
    hp = _modulated_norm(xp_ref[...], mod, g[0:1]) * (i > 0).astype(F32)
    hn = _modulated_norm(xn_ref[...], mod, g[0:1]) * (i < n_i - 1).astype(F32)
    h_ref[0:POOL_HALO, :] = hp
    h_ref[POOL_HALO:POOL_HALO + rows, :] = h
    h_ref[POOL_HALO + rows:, :] = hn

    t = i * rows + lax.broadcasted_iota(jnp.int32, (rows, 1), 0)
    ys = []
    for gi, window in enumerate(POOL_WINDOWS):
        half = window // 2
        lanes = slice(gi * POOL_GROUP, (gi + 1) * POOL_GROUP)
        tot = h_ref[POOL_HALO - half:POOL_HALO - half + rows, lanes]
        for d in range(-half + 1, half):
            tot = tot + h_ref[POOL_HALO + d:POOL_HALO + d + rows, lanes]
        cnt = jnp.minimum(t + half, seq) - jnp.maximum(t - half, 0)
        pooled = tot / cnt.astype(F32)
        diff = (pooled - h[:, lanes]).astype(BF16)
        ys.append(jnp.dot(diff, w_ref[gi], preferred_element_type=F32))
    y = (jnp.concatenate(ys, axis=-1) + b_ref[...]) * s_ref[...]
    o_ref[...] = _residual(x, y, mod, g[1:2], 1.0)


def _pool_sublayer(x, mod, sub, g_pair, pool_w, pool_b, pool_scale):
    batch, seq, _ = x.shape
    halo_blocks = POOL_ROWS // POOL_HALO
    n_halo = seq // POOL_HALO
    return pl.pallas_call(
        functools.partial(_pool_kernel, seq=seq),
        grid=(batch, seq // POOL_ROWS),
        in_specs=[
            pl.BlockSpec((None, POOL_ROWS, D_MODEL), lambda b, i: (b, i, 0)),
            pl.BlockSpec((None, POOL_HALO, D_MODEL),
                         lambda b, i: (b, jnp.maximum(i * halo_blocks - 1, 0), 0)),
            pl.BlockSpec((None, POOL_HALO, D_MODEL),
                         lambda b, i: (b, jnp.minimum((i + 1) * halo_blocks, n_halo - 1), 0)),
            pl.BlockSpec((None, None, 3, D_MODEL), lambda b, i: (b, sub, 0, 0)),
            pl.BlockSpec((2, D_MODEL), lambda b, i: (0, 0)),
            pl.BlockSpec((N_POOL_GROUPS, POOL_GROUP, POOL_GROUP), lambda b, i: (0, 0, 0)),
            pl.BlockSpec((1, D_MODEL), lambda b, i: (0, 0)),
            pl.BlockSpec((1, D_MODEL), lambda b, i: (0, 0)),
        ],
        out_specs=pl.BlockSpec((None, POOL_ROWS, D_MODEL), lambda b, i: (b, i, 0)),
        out_shape=jax.ShapeDtypeStruct(x.shape, F32),
        scratch_shapes=[pltpu.VMEM((POOL_ROWS + 2 * POOL_HALO, D_MODEL), F32)],
        compiler_params=_params(2),
        name="pool_sublayer",
    )(x, x, x, mod, g_pair, pool_w.astype(BF16), pool_b.reshape(1, D_MODEL),
      pool_scale.reshape(1, D_MODEL))


LAT_COLS = Q_LORA + KV_LORA + 2 * LANES


def _fold_qk_kernel(wuq_ref, wuk_ref, o_ref):
    o_ref[...] = Q_SCALE * lax.dot_general(
        wuq_ref[:, :QK_NOPE], wuk_ref[...], (((1,), (1,)), ((), ())),
        preferred_element_type=F32, precision=lax.Precision.HIGHEST)


def _mla_weights(w_in, w_uq, w_uk, w_uv, w_o):
    wuq_h = w_uq.transpose(1, 0, 2)
    wuk_h = w_uk.transpose(1, 0, 2)
    w_ql = pl.pallas_call(
        _fold_qk_kernel,
        grid=(N_HEADS,),
        in_specs=[
            pl.BlockSpec((None, Q_LORA, QK_NOPE + QK_ROPE), lambda h: (h, 0, 0)),
            pl.BlockSpec((None, KV_LORA, QK_NOPE), lambda h: (h, 0, 0)),
        ],
        out_specs=pl.BlockSpec((None, Q_LORA, KV_LORA), lambda h: (h, 0, 0)),
        out_shape=jax.ShapeDtypeStruct((N_HEADS, Q_LORA, KV_LORA), F32),
        compiler_params=_params(1),
        name="fold_qk",
    )(wuq_h, wuk_h)

    def rope_cols(w, swap):
        a, b = w[..., :HALF_ROPE], w[..., HALF_ROPE:]
        first, second = (b, a) if swap else (a, b)
        pad = jnp.zeros(w.shape[:-1] + (LANES - QK_ROPE,), w.dtype)
        return jnp.concatenate([first, second, pad], axis=-1)

    wqt = jnp.concatenate([w_ql, wuq_h[..., QK_NOPE:]], axis=-1).transpose(0, 2, 1).astype(BF16)
    wk_rope = w_in[:, Q_LORA + KV_LORA:]
    w_lat = jnp.concatenate(
        [w_in[:, :Q_LORA + KV_LORA], rope_cols(wk_rope, False), rope_cols(wk_rope, True)],
        axis=-1).astype(BF16)
    w_uvt = w_uv.transpose(1, 2, 0).astype(BF16)
    return w_lat, wqt, w_uvt, w_o.T.astype(BF16)


def _rope_tables(seq):
    inv = 1.0 / (ROPE_THETA ** (np.arange(0, QK_ROPE, 2, dtype=np.float64) / QK_ROPE))
    ang = np.arange(seq, dtype=np.float64)[:, None] * inv[None, :]
    cos, sin = np.cos(ang), np.sin(ang)
    pad = np.zeros((seq, LANES - QK_ROPE))
    as_f32 = lambda a: jnp.asarray(a.astype(np.float32))
    return (as_f32(np.concatenate([cos, cos, pad], axis=-1)),
            as_f32(np.concatenate([-sin, sin, pad], axis=-1)),
            as_f32(cos.T * Q_SCALE), as_f32(sin.T * Q_SCALE))


def _mla_pre_kernel(x_ref, mod_ref, g_ref, wlat_ref, qn_ref, kvn_ref, wqt_ref,
                    cosk_ref, sink_ref, cosq_ref, sinq_ref, qt_ref, k_ref, vt_ref):
    x = x_ref[...]
    rows = x.shape[0]
    hb = _modulated_norm(x, mod_ref[...], g_ref[0:1, :]).astype(BF16)
    lat = jnp.dot(hb, wlat_ref[...], preferred_element_type=F32)
    c_q = _rms(lat[:, :Q_LORA]) * qn_ref[...]
    c_kv = _rms(lat[:, Q_LORA:Q_LORA + KV_LORA]) * kvn_ref[...]
    k_rope = (lat[:, Q_LORA + KV_LORA:Q_LORA + KV_LORA + LANES] * cosk_ref[...]
              + lat[:, Q_LORA + KV_LORA + LANES:] * sink_ref[...])
    k_ref[...] = jnp.concatenate([c_kv, k_rope], axis=-1).astype(BF16)
    vt_ref[0:KV_LORA, :] = c_kv.T.astype(BF16)
    ones_row = lax.broadcasted_iota(jnp.int32, (VT_ROWS - KV_LORA, rows), 0) == 0
    vt_ref[KV_LORA:, :] = ones_row.astype(BF16)
    cqt = c_q.T.astype(BF16)
    cos_q = cosq_ref[...]
    sin_q = sinq_ref[...]
    r1 = KV_LORA + HALF_ROPE
    zero_pad = jnp.zeros((QK_WIDTH - QK_FEATS, rows), BF16)
    for head in range(N_HEADS):
        a = jnp.dot(wqt_ref[head], cqt, preferred_element_type=F32)
        x1 = a[KV_LORA:r1]
        x2 = a[r1:]
        qt_ref[head, 0:KV_LORA, :] = a[:KV_LORA].astype(BF16)
        qt_ref[head, KV_LORA:r1, :] = (x1 * cos_q - x2 * sin_q).astype(BF16)
        qt_ref[head, r1:QK_FEATS, :] = (x2 * cos_q + x1 * sin_q).astype(BF16)
        qt_ref[head, QK_FEATS:, :] = zero_pad


def _mla_pre(x, mod, sub, g_pair, w_lat, q_norm, kv_norm, wqt, tables):
    batch, seq, _ = x.shape
    rows = MLA_PRE_ROWS
    return pl.pallas_call(
        _mla_pre_kernel,
        grid=(batch, seq // rows),
        in_specs=[
            pl.BlockSpec((None, rows, D_MODEL), lambda b, i: (b, i, 0)),
            pl.BlockSpec((None, None, 3, D_MODEL), lambda b, i: (b, sub, 0, 0)),
            pl.BlockSpec((2, D_MODEL), lambda b, i: (0, 0)),
            pl.BlockSpec((D_MODEL, LAT_COLS), lambda b, i: (0, 0)),
            pl.BlockSpec((1, Q_LORA), lambda b, i: (0, 0)),
            pl.BlockSpec((1, KV_LORA), lambda b, i: (0, 0)),
            pl.BlockSpec((N_HEADS, QK_FEATS, Q_LORA), lambda b, i: (0, 0, 0)),
            pl.BlockSpec((rows, LANES), lambda b, i: (i, 0)),
            pl.BlockSpec((rows, LANES), lambda b, i: (i, 0)),
            pl.BlockSpec((HALF_ROPE, rows), lambda b, i: (0, i)),
            pl.BlockSpec((HALF_ROPE, rows), lambda b, i: (0, i)),
        ],
        out_specs=[
            pl.BlockSpec((None, N_HEADS, QK_WIDTH, rows), lambda b, i: (b, 0, 0, i)),
            pl.BlockSpec((None, rows, QK_WIDTH), lambda b, i: (b, i, 0)),
            pl.BlockSpec((None, VT_ROWS, rows), lambda b, i: (b, 0, i)),
        ],
        out_shape=[
            jax.ShapeDtypeStruct((batch, N_HEADS, QK_WIDTH, seq), BF16),
            jax.ShapeDtypeStruct((batch, seq, QK_WIDTH), BF16),
            jax.ShapeDtypeStruct((batch, VT_ROWS, seq), BF16),
        ],
        compiler_params=_params(2),
        name="mla_pre",
    )(x, mod, g_pair, w_lat, q_norm.reshape(1, Q_LORA), kv_norm.reshape(1, KV_LORA),
      wqt, *tables)


def _mla_attn_kernel(qt_ref, k_ref, vt_ref, x_ref, mod_ref, g_ref, wuvt_ref, wot_ref, o_ref,
                     st_ref, m_ref, acc_ref, ov_ref):
    seq = k_ref.shape[0]
    n_tiles = seq // KEY_TILE

    def score_tile(qt_next, slot, kt, mx):
        keys = pl.ds(kt * KEY_TILE, KEY_TILE)
        st = jnp.dot(k_ref[keys, :], qt_next, preferred_element_type=F32)
        st_ref[slot, keys, :] = st
        tmax = jnp.max(st.reshape(KEY_TILE // SUBLANES, SUBLANES, st.shape[1]), axis=0)
        return tmax if mx is None else jnp.maximum(mx, tmax)

    def prob_tile(slot, kt, m, acc):
        keys = pl.ds(kt * KEY_TILE, KEY_TILE)
        pt = jnp.exp2(st_ref[slot, keys, :] - m).astype(BF16)
        part = jnp.dot(vt_ref[:, keys], pt, preferred_element_type=F32)
        return part if acc is None else acc + part

    def scores(head, slot):
        qt_next = qt_ref[head]
        mx = None
        for kt in range(n_tiles):
            mx = score_tile(qt_next, slot, kt, mx)
        m_ref[slot] = jnp.max(mx, axis=0, keepdims=True)

    def emit(head, slot):
        acc = acc_ref[slot]
        ot = acc[:KV_LORA] * (1.0 / acc[KV_LORA:KV_LORA + 1])
        ovt = jnp.dot(wuvt_ref[head], ot.astype(BF16), preferred_element_type=F32)
        ov_ref[pl.ds(pl.multiple_of(head * V_HEAD, V_HEAD), V_HEAD), :] = ovt.astype(BF16)

    def overlapped(head, slot, emit_previous=True, score_next=True):
        qt_next = qt_ref[head + 1] if score_next else None
        m = m_ref[slot]
        acc = None
        mx = None
        for kt in range(n_tiles + SCORE_LEAD):
            if score_next and kt < n_tiles:
                mx = score_tile(qt_next, 1 - slot, kt, mx)
            if emit_previous and kt == SCORE_LEAD - 1:
                emit(head - 1, 1 - slot)
            if kt >= SCORE_LEAD:
                acc = prob_tile(slot, kt - SCORE_LEAD, m, acc)
        if score_next:
            m_ref[1 - slot] = jnp.max(mx, axis=0, keepdims=True)
        acc_ref[slot] = acc

    def pair_body(j, carry):
        overlapped(2 * j + 1, 1)
        overlapped(2 * j + 2, 0)
        return carry

    scores(0, 0)
    overlapped(0, 0, emit_previous=False)
    lax.fori_loop(0, N_HEADS // 2 - 1, pair_body, 0)
    overlapped(N_HEADS - 1, 1, score_next=False)
    emit(N_HEADS - 1, 1)
    yt = jnp.dot(wot_ref[...], ov_ref[...], preferred_element_type=F32)
    o_ref[...] = _residual(x_ref[...], yt.T, mod_ref[...], g_ref[1:2, :], 1.0)


def _mla_attn(x, qt, k, vt, mod, sub, g_pair, w_uvt, w_ot):
    batch, seq, _ = x.shape
    rows = ATTN_ROWS
    return pl.pallas_call(
        _mla_attn_kernel,
        grid=(batch, seq // rows),
        in_specs=[
            pl.BlockSpec((None, N_HEADS, QK_WIDTH, rows), lambda b, i: (b, 0, 0, i)),
            pl.BlockSpec((None, seq, QK_WIDTH), lambda b, i: (b, 0, 0)),
            pl.BlockSpec((None, VT_ROWS, seq), lambda b, i: (b, 0, 0)),
            pl.BlockSpec((None, rows, D_MODEL), lambda b, i: (b, i, 0)),
            pl.BlockSpec((None, None, 3, D_MODEL), lambda b, i: (b, sub, 0, 0)),
            pl.BlockSpec((2, D_MODEL), lambda b, i: (0, 0)),
            _resident((N_HEADS, V_HEAD, KV_LORA), lambda b, i: (0, 0, 0)),
            _resident((D_MODEL, N_HEADS * V_HEAD), lambda b, i: (0, 0)),
        ],
        out_specs=pl.BlockSpec((None, rows, D_MODEL), lambda b, i: (b, i, 0)),
        out_shape=jax.ShapeDtypeStruct(x.shape, F32),
        scratch_shapes=[
            pltpu.VMEM((2, seq, rows), F32),
            pltpu.VMEM((2, 1, rows), F32),
            pltpu.VMEM((2, VT_ROWS, rows), F32),
            pltpu.VMEM((N_HEADS * V_HEAD, rows), BF16),
        ],
        compiler_params=_params(2),
        name="mla_attn",
    )(qt, k, vt, x, mod, g_pair, w_uvt, w_ot)


def kernel(x, c, ada_w, ada_b, norm_g, ffn_w_in, ffn_w_out, pool_w, pool_b, pool_scale,
           mla_w_in, mla_q_norm, mla_kv_norm, mla_w_uq, mla_w_uk, mla_w_uv, mla_w_o):
    batch, seq, _ = x.shape
    mods = _ada_mod(c, ada_w, ada_b).reshape(DEPTH, batch, 3, 3, D_MODEL)
    tables = _rope_tables(seq)
    for i in range(DEPTH):
        mod = mods[i]
        g = norm_g[i]
        li = i // 2
        x = _ffn_sublayer(x, mod, 0, g[0:2], *_ffn_weights(ffn_w_in[i, 0], ffn_w_out[i, 0]))
        if i % 2 == 0:
            x = _pool_sublayer(x, mod, 1, g[2:4], pool_w[li], pool_b[li], pool_scale[li])
        else:
            w_lat, wqt, w_uvt, w_ot = _mla_weights(
                mla_w_in[li], mla_w_uq[li], mla_w_uk[li], mla_w_uv[li], mla_w_o[li])
            qt, k, vt = _mla_pre(x, mod, 1, g[2:4], w_lat, mla_q_norm[li], mla_kv_norm[li],
                                 wqt, tables)
            x = _mla_attn(x, qt, k, vt, mod, 1, g[2:4], w_uvt, w_ot)
        x = _ffn_sublayer(x, mod, 2, g[4:6], *_ffn_weights(ffn_w_in[i, 1], ffn_w_out[i, 1]))
    return x
```

```python
import functools

import jax
import jax.numpy as jnp
import numpy as np
from jax import lax
from jax.experimental import pallas as pl
from jax.experimental.pallas import tpu as pltpu

F32 = jnp.float32
BF16 = jnp.bfloat16

D_MODEL = 1024
DEPTH = 2
N_POOL_GROUPS = 4
POOL_GROUP = D_MODEL // N_POOL_GROUPS
POOL_WINDOWS = (2, 4, 8, 16)
N_HEADS = 16
QK_NOPE = 64
QK_ROPE = 32
V_HEAD = 64
Q_LORA = D_MODEL // 4
KV_LORA = D_MODEL // 8
ROPE_THETA = 10000.0
D_FF = 11 * D_MODEL // 4
EPS = 1e-6
N_MOD = 9
ATTN_SCALE = (QK_NOPE + QK_ROPE) ** -0.5
LOG2_E = 1.4426950408889634
Q_SCALE = ATTN_SCALE * LOG2_E

LANES = 128
SUBLANES = 8
MXU_DIM = 256
VMEM_LIMIT_BYTES = 56 * 1024 * 1024

FFN_CHUNK = MXU_DIM
N_FFN_CHUNKS = D_FF // FFN_CHUNK
FFN_ROWS = 1024
FFN_SUB_ROWS = 256
POOL_ROWS = 512
POOL_HALO = SUBLANES
MLA_PRE_ROWS = 512
ATTN_ROWS = 512
KEY_TILE = MXU_DIM
SCORE_LEAD = 2
PROJ_HEADS = MXU_DIM // V_HEAD
LAST_HEAD_PROJ_AT = (3, 8, 13)
QK_WIDTH = 2 * LANES
HALF_ROPE = QK_ROPE // 2
QK_FEATS = KV_LORA + QK_ROPE
BF16_SUBLANES = 2 * SUBLANES
VT_ROWS = KV_LORA + BF16_SUBLANES


def _params(n_axes, flags=None):
    return pltpu.CompilerParams(
        dimension_semantics=("arbitrary",) * n_axes,
        vmem_limit_bytes=VMEM_LIMIT_BYTES,
        flags=flags,
    )


def _resident(block_shape, index_map):
    return pl.BlockSpec(block_shape, index_map, pipeline_mode=pl.Buffered(1))


def _rms(v):
    return v * lax.rsqrt(jnp.mean(v * v, axis=-1, keepdims=True) + EPS)


def _modulated_norm(x, mod, g_pre):
    return _rms(x) * (g_pre * (1.0 + mod[1:2])) + mod[0:1]


def _residual(x, y, mod, g_post, weight):
    return x + (weight * (1.0 + mod[2:3])) * (_rms(y) * g_post)


ADA_COLS = 1152


def _ada_kernel(c_ref, w_ref, b_ref, o_ref):
    c = c_ref[...]
    sc = c * (1.0 / (1.0 + jnp.exp(-c)))
    o_ref[...] = jnp.dot(sc.astype(BF16), w_ref[...].astype(BF16),
                         preferred_element_type=F32) + b_ref[...]


def _ada_mod(c, ada_w, ada_b):
    batch = c.shape[0]
    n_out = N_MOD * D_MODEL
    return pl.pallas_call(
        _ada_kernel,
        grid=(DEPTH, n_out // ADA_COLS),
        in_specs=[
            pl.BlockSpec((batch, D_MODEL), lambda l, j: (0, 0)),
            pl.BlockSpec((None, D_MODEL, ADA_COLS), lambda l, j: (l, 0, j)),
            pl.BlockSpec((None, 1, ADA_COLS), lambda l, j: (l, 0, j)),
        ],
        out_specs=pl.BlockSpec((None, batch, ADA_COLS), lambda l, j: (l, 0, j)),
        out_shape=jax.ShapeDtypeStruct((DEPTH, batch, n_out), F32),
        compiler_params=_params(2),
        name="ada_mod",
    )(c, ada_w, ada_b.reshape(DEPTH, 1, n_out))


def _ffn_kernel(x_ref, mod_ref, g_ref, win_ref, wout_ref, o_ref, act_ref):
    mod = mod_ref[...]
    g = g_ref[...]
    n_sub = x_ref.shape[0] // FFN_SUB_ROWS

    def rows_of(s):
        return pl.ds(s * FFN_SUB_ROWS, FFN_SUB_ROWS)

    def normed(s):
        return _modulated_norm(x_ref[rows_of(s), :], mod, g[0:1]).astype(BF16)

    def finish(s):
        o_ref[rows_of(s), :] = _residual(x_ref[rows_of(s), :], o_ref[rows_of(s), :], mod,
                                         g[1:2], 0.5)

    hb = normed(0)
    for s in range(n_sub):
        hb_next = None
        for c in range(N_FFN_CHUNKS):
            gu = jnp.dot(hb, win_ref[:, 2 * c * FFN_CHUNK:2 * (c + 1) * FFN_CHUNK],
                         preferred_element_type=F32)
            gate = gu[:, :FFN_CHUNK]
            up = gu[:, FFN_CHUNK:]
            act = gate * (1.0 / (1.0 + jnp.exp(-gate))) * up
            act_ref[s % 2, :, c * FFN_CHUNK:(c + 1) * FFN_CHUNK] = act.astype(BF16)
            if c == 3 and s + 1 < n_sub:
                hb_next = normed(s + 1)
            if c == 7 and s >= 1:
                finish(s - 1)
        o_ref[rows_of(s), :] = jnp.dot(act_ref[s % 2], wout_ref[...], preferred_element_type=F32)
        hb = hb_next
    finish(n_sub - 1)


def _ffn_sublayer(x, mod, sub, g_pair, w_in_c, w_out_b):
    batch, seq, _ = x.shape
    return pl.pallas_call(
        _ffn_kernel,
        grid=(batch, seq // FFN_ROWS),
        in_specs=[
            pl.BlockSpec((None, FFN_ROWS, D_MODEL), lambda b, i: (b, i, 0)),
            pl.BlockSpec((None, None, 3, D_MODEL), lambda b, i: (b, sub, 0, 0)),
            pl.BlockSpec((2, D_MODEL), lambda b, i: (0, 0)),
            _resident((D_MODEL, 2 * D_FF), lambda b, i: (0, 0)),
            _resident((D_FF, D_MODEL), lambda b, i: (0, 0)),
        ],
        out_specs=pl.BlockSpec((None, FFN_ROWS, D_MODEL), lambda b, i: (b, i, 0)),
        out_shape=jax.ShapeDtypeStruct(x.shape, F32),
        scratch_shapes=[pltpu.VMEM((2, FFN_SUB_ROWS, D_FF), BF16)],
        compiler_params=_params(2),
        name="ffn_sublayer",
    )(x, mod, g_pair, w_in_c, w_out_b)


def _ffn_weights(w_in, w_out):
    gate = w_in[:, :D_FF].reshape(D_MODEL, N_FFN_CHUNKS, FFN_CHUNK)
    up = w_in[:, D_FF:].reshape(D_MODEL, N_FFN_CHUNKS, FFN_CHUNK)
    w_in_c = jnp.concatenate([gate, up], axis=-1).reshape(D_MODEL, 2 * D_FF).astype(BF16)
    return w_in_c, w_out.astype(BF16)


def _pool_kernel(x_ref, xp_ref, xn_ref, mod_ref, g_ref, w_ref, b_ref, s_ref, o_ref, h_ref,
                 *, seq):
    i = pl.program_id(1)
    n_i = pl.num_programs(1)
    x = x_ref[...]
    mod = mod_ref[...]
    g = g_ref[...]
    rows = x.shape[0]
    h = _modulated_norm(x, mod, g[0:1])
    hp = _modulated_norm(xp_ref[...], mod, g[0:1]) * (i > 0).astype(F32)
    hn = _modulated_norm(xn_ref[...], mod, g[0:1]) * (i < n_i - 1).astype(F32)
    h_ref[0:POOL_HALO, :] = hp
    h_ref[POOL_HALO:POOL_HALO + rows, :] = h
    h_ref[POOL_HALO + rows:, :] = hn

    t = i * rows + lax.broadcasted_iota(jnp.int32, (rows, 1), 0)
    ys = []
    for gi, window in enumerate(POOL_WINDOWS):
        half = window // 2
        lanes = slice(gi * POOL_GROUP, (gi + 1) * POOL_GROUP)
        tot = h_ref[POOL_HALO - half:POOL_HALO - half + rows, lanes]
        for d in range(-half + 1, half):
            tot = tot + h_ref[POOL_HALO + d:POOL_HALO + d + rows, lanes]
        cnt = jnp.minimum(t + half, seq) - jnp.maximum(t - half, 0)
        pooled = tot / cnt.astype(F32)
        diff = (pooled - h[:, lanes]).astype(BF16)
        ys.append(jnp.dot(diff, w_ref[gi], preferred_element_type=F32))
    y = (jnp.concatenate(ys, axis=-1) + b_ref[...]) * s_ref[...]
    o_ref[...] = _residual(x, y, mod, g[1:2], 1.0)


def _pool_sublayer(x, mod, sub, g_pair, pool_w, pool_b, pool_scale):
    batch, seq, _ = x.shape
    halo_blocks = POOL_ROWS // POOL_HALO
    n_halo = seq // POOL_HALO
    return pl.pallas_call(
        functools.partial(_pool_kernel, seq=seq),
        grid=(batch, seq // POOL_ROWS),
        in_specs=[
            pl.BlockSpec((None, POOL_ROWS, D_MODEL), lambda b, i: (b, i, 0)),
            pl.BlockSpec((None, POOL_HALO, D_MODEL),
                         lambda b, i: (b, jnp.maximum(i * halo_blocks - 1, 0), 0)),
            pl.BlockSpec((None, POOL_HALO, D_MODEL),
                         lambda b, i: (b, jnp.minimum((i + 1) * halo_blocks, n_halo - 1), 0)),
            pl.BlockSpec((None, None, 3, D_MODEL), lambda b, i: (b, sub, 0, 0)),
            pl.BlockSpec((2, D_MODEL), lambda b, i: (0, 0)),
            pl.BlockSpec((N_POOL_GROUPS, POOL_GROUP, POOL_GROUP), lambda b, i: (0, 0, 0)),
            pl.BlockSpec((1, D_MODEL), lambda b, i: (0, 0)),
            pl.BlockSpec((1, D_MODEL), lambda b, i: (0, 0)),
        ],
        out_specs=pl.BlockSpec((None, POOL_ROWS, D_MODEL), lambda b, i: (b, i, 0)),
        out_shape=jax.ShapeDtypeStruct(x.shape, F32),
        scratch_shapes=[pltpu.VMEM((POOL_ROWS + 2 * POOL_HALO, D_MODEL), F32)],
        compiler_params=_params(2),
        name="pool_sublayer",
    )(x, x, x, mod, g_pair, pool_w.astype(BF16), pool_b.reshape(1, D_MODEL),
      pool_scale.reshape(1, D_MODEL))


LAT_COLS = Q_LORA + KV_LORA + 2 * LANES


def _fold_qk_kernel(wuq_ref, wuk_ref, o_ref):
    o_ref[...] = Q_SCALE * lax.dot_general(
        wuq_ref[:, :QK_NOPE], wuk_ref[...], (((1,), (1,)), ((), ())),
        preferred_element_type=F32, precision=lax.Precision.HIGHEST)


def _mla_weights(w_in, w_uq, w_uk, w_uv, w_o):
    wuq_h = w_uq.transpose(1, 0, 2)
    wuk_h = w_uk.transpose(1, 0, 2)
    w_ql = pl.pallas_call(
        _fold_qk_kernel,
        grid=(N_HEADS,),
        in_specs=[
            pl.BlockSpec((None, Q_LORA, QK_NOPE + QK_ROPE), lambda h: (h, 0, 0)),
            pl.BlockSpec((None, KV_LORA, QK_NOPE), lambda h: (h, 0, 0)),
        ],
        out_specs=pl.BlockSpec((None, Q_LORA, KV_LORA), lambda h: (h, 0, 0)),
        out_shape=jax.ShapeDtypeStruct((N_HEADS, Q_LORA, KV_LORA), F32),
        compiler_params=_params(1),
        name="fold_qk",
    )(wuq_h, wuk_h)

    def rope_cols(w, swap):
        a, b = w[..., :HALF_ROPE], w[..., HALF_ROPE:]
        first, second = (b, a) if swap else (a, b)
        pad = jnp.zeros(w.shape[:-1] + (LANES - QK_ROPE,), w.dtype)
        return jnp.concatenate([first, second, pad], axis=-1)

    wqt = jnp.concatenate([w_ql, wuq_h[..., QK_NOPE:]], axis=-1).transpose(0, 2, 1).astype(BF16)
    wk_rope = w_in[:, Q_LORA + KV_LORA:]
    w_lat = jnp.concatenate(
        [w_in[:, :Q_LORA + KV_LORA], rope_cols(wk_rope, False), rope_cols(wk_rope, True)],
        axis=-1).astype(BF16)
    w_uvt = w_uv.transpose(1, 2, 0).astype(BF16)
    return w_lat, wqt, w_uvt, w_o.astype(BF16)


def _rope_tables(seq):
    inv = 1.0 / (ROPE_THETA ** (np.arange(0, QK_ROPE, 2, dtype=np.float64) / QK_ROPE))
    ang = np.arange(seq, dtype=np.float64)[:, None] * inv[None, :]
    cos, sin = np.cos(ang), np.sin(ang)
    pad = np.zeros((seq, LANES - QK_ROPE))
    as_f32 = lambda a: jnp.asarray(a.astype(np.float32))
    return (as_f32(np.concatenate([cos, cos, pad], axis=-1)),
            as_f32(np.concatenate([-sin, sin, pad], axis=-1)),
            as_f32(cos.T * Q_SCALE), as_f32(sin.T * Q_SCALE))


def _mla_pre_kernel(x_ref, mod_ref, g_ref, wlat_ref, qn_ref, kvn_ref, wqt_ref,
                    cosk_ref, sink_ref, cosq_ref, sinq_ref, qt_ref, k_ref, vt_ref):
    x = x_ref[...]
    rows = x.shape[0]
    hb = _modulated_norm(x, mod_ref[...], g_ref[0:1, :]).astype(BF16)
    lat = jnp.dot(hb, wlat_ref[...], preferred_element_type=F32)
    c_q = _rms(lat[:, :Q_LORA]) * qn_ref[...]
    c_kv = _rms(lat[:, Q_LORA:Q_LORA + KV_LORA]) * kvn_ref[...]
    k_rope = (lat[:, Q_LORA + KV_LORA:Q_LORA + KV_LORA + LANES] * cosk_ref[...]
              + lat[:, Q_LORA + KV_LORA + LANES:] * sink_ref[...])
    k_ref[...] = jnp.concatenate([c_kv, k_rope], axis=-1).astype(BF16)
    vt_ref[0:KV_LORA, :] = c_kv.T.astype(BF16)
    ones_row = lax.broadcasted_iota(jnp.int32, (VT_ROWS - KV_LORA, rows), 0) == 0
    vt_ref[KV_LORA:, :] = ones_row.astype(BF16)
    cqt = c_q.T.astype(BF16)
    cos_q = cosq_ref[...]
    sin_q = sinq_ref[...]
    r1 = KV_LORA + HALF_ROPE
    for head in range(N_HEADS):
        a = jnp.dot(wqt_ref[head], cqt, preferred_element_type=F32)
        x1 = a[KV_LORA:r1]
        x2 = a[r1:]
        qt_ref[head, 0:KV_LORA, :] = a[:KV_LORA].astype(BF16)
        qt_ref[head, KV_LORA:r1, :] = (x1 * cos_q - x2 * sin_q).astype(BF16)
        qt_ref[head, r1:QK_FEATS, :] = (x2 * cos_q + x1 * sin_q).astype(BF16)


def _mla_pre(x, mod, sub, g_pair, w_lat, q_norm, kv_norm, wqt, tables):
    batch, seq, _ = x.shape
    rows = MLA_PRE_ROWS
    return pl.pallas_call(
        _mla_pre_kernel,
        grid=(batch, seq // rows),
        in_specs=[
            pl.BlockSpec((None, rows, D_MODEL), lambda b, i: (b, i, 0)),
            pl.BlockSpec((None, None, 3, D_MODEL), lambda b, i: (b, sub, 0, 0)),
            pl.BlockSpec((2, D_MODEL), lambda b, i: (0, 0)),
            pl.BlockSpec((D_MODEL, LAT_COLS), lambda b, i: (0, 0)),
            pl.BlockSpec((1, Q_LORA), lambda b, i: (0, 0)),
            pl.BlockSpec((1, KV_LORA), lambda b, i: (0, 0)),
            pl.BlockSpec((N_HEADS, QK_FEATS, Q_LORA), lambda b, i: (0, 0, 0)),
            pl.BlockSpec((rows, LANES), lambda b, i: (i, 0)),
            pl.BlockSpec((rows, LANES), lambda b, i: (i, 0)),
            pl.BlockSpec((HALF_ROPE, rows), lambda b, i: (0, i)),
            pl.BlockSpec((HALF_ROPE, rows), lambda b, i: (0, i)),
        ],
        out_specs=[
            pl.BlockSpec((None, N_HEADS, QK_FEATS, rows), lambda b, i: (b, 0, 0, i)),
            pl.BlockSpec((None, rows, QK_WIDTH), lambda b, i: (b, i, 0)),
            pl.BlockSpec((None, VT_ROWS, rows), lambda b, i: (b, 0, i)),
        ],
        out_shape=[
            jax.ShapeDtypeStruct((batch, N_HEADS, QK_FEATS, seq), BF16),
            jax.ShapeDtypeStruct((batch, seq, QK_WIDTH), BF16),
            jax.ShapeDtypeStruct((batch, VT_ROWS, seq), BF16),
        ],
        compiler_params=_params(2),
        name="mla_pre",
    )(x, mod, g_pair, w_lat, q_norm.reshape(1, Q_LORA), kv_norm.reshape(1, KV_LORA),
      wqt, *tables)


def _mla_attn_kernel(qt_ref, k_ref, vt_ref, x_ref, mod_ref, g_ref, wuvt_ref, wo_ref, o_ref,
                     st_ref, m_ref, acc_ref, ov_ref):
    seq = k_ref.shape[0]
    n_tiles = seq // KEY_TILE

    def score_tile(qt_next, slot, kt, mx):
        keys = pl.ds(kt * KEY_TILE, KEY_TILE)
        st = jnp.dot(k_ref[keys, 0:QK_FEATS], qt_next, preferred_element_type=F32)
        st_ref[slot, keys, :] = st
        tmax = jnp.max(st.reshape(KEY_TILE // SUBLANES, SUBLANES, st.shape[1]), axis=0)
        return tmax if mx is None else jnp.maximum(mx, tmax)

    def prob_tile(slot, kt, m, acc):
        keys = pl.ds(kt * KEY_TILE, KEY_TILE)
        pt = jnp.exp2(st_ref[slot, keys, :] - m).astype(BF16)
        part = jnp.dot(vt_ref[:, keys], pt, preferred_element_type=F32)
        return part if acc is None else acc + part

    def scores(head, slot):
        qt_next = qt_ref[head]
        mx = None
        for kt in range(n_tiles):
            mx = score_tile(qt_next, slot, kt, mx)
        m_ref[slot] = jnp.max(mx, axis=0, keepdims=True)

    def emit(head, slot):
        acc = acc_ref[slot]
        ot = acc[:KV_LORA] * (1.0 / acc[KV_LORA:KV_LORA + 1])
        ovt = jnp.dot(wuvt_ref[head], ot.astype(BF16), preferred_element_type=F32)
        ov_ref[pl.ds(pl.multiple_of(head * V_HEAD, V_HEAD), V_HEAD), :] = ovt.astype(BF16)

    def out_proj_chunk(c, y):
        feats = pl.ds(c * PROJ_HEADS * V_HEAD, PROJ_HEADS * V_HEAD)
        part = lax.dot_general(ov_ref[feats, :], wo_ref[feats, :], (((0,), (0,)), ((), ())),
                               preferred_element_type=F32)
        return part if y is None else y + part

    def overlapped(head, slot, emit_previous=True, score_next=True, proj_at=()):
        qt_next = qt_ref[head + 1] if score_next else None
        m = m_ref[slot]
        acc = None
        mx = None
        y = None
        for kt in range(n_tiles + SCORE_LEAD):
            if score_next and kt < n_tiles:
                mx = score_tile(qt_next, 1 - slot, kt, mx)
            if emit_previous and kt == SCORE_LEAD - 1:
                emit(head - 1, 1 - slot)
            if kt >= SCORE_LEAD:
                acc = prob_tile(slot, kt - SCORE_LEAD, m, acc)
            if kt in proj_at:
                y = out_proj_chunk(proj_at.index(kt), y)
        if score_next:
            m_ref[1 - slot] = jnp.max(mx, axis=0, keepdims=True)
        acc_ref[slot] = acc
        return y

    def pair_body(j, carry):
        overlapped(2 * j + 1, 1)
        overlapped(2 * j + 2, 0)
        return carry

    scores(0, 0)
    overlapped(0, 0, emit_previous=False)
    lax.fori_loop(0, N_HEADS // 2 - 1, pair_body, 0)
    n_proj = N_HEADS // PROJ_HEADS
    y = overlapped(N_HEADS - 1, 1, score_next=False, proj_at=LAST_HEAD_PROJ_AT[:n_proj - 1])
    emit(N_HEADS - 1, 1)
    y = out_proj_chunk(n_proj - 1, y)
    o_ref[...] = _residual(x_ref[...], y, mod_ref[...], g_ref[1:2, :], 1.0)


def _mla_attn(x, qt, k, vt, mod, sub, g_pair, w_uvt, w_ot):
    batch, seq, _ = x.shape
    rows = ATTN_ROWS
    return pl.pallas_call(
        _mla_attn_kernel,
        grid=(batch, seq // rows),
        in_specs=[
            pl.BlockSpec((None, N_HEADS, QK_FEATS, rows), lambda b, i: (b, 0, 0, i)),
            pl.BlockSpec((None, seq, QK_WIDTH), lambda b, i: (b, 0, 0)),
            pl.BlockSpec((None, VT_ROWS, seq), lambda b, i: (b, 0, 0)),
            pl.BlockSpec((None, rows, D_MODEL), lambda b, i: (b, i, 0)),
            pl.BlockSpec((None, None, 3, D_MODEL), lambda b, i: (b, sub, 0, 0)),
            pl.BlockSpec((2, D_MODEL), lambda b, i: (0, 0)),
            _resident((N_HEADS, V_HEAD, KV_LORA), lambda b, i: (0, 0, 0)),
            _resident((D_MODEL, N_HEADS * V_HEAD), lambda b, i: (0, 0)),
        ],
        out_specs=pl.BlockSpec((None, rows, D_MODEL), lambda b, i: (b, i, 0)),
        out_shape=jax.ShapeDtypeStruct(x.shape, F32),
        scratch_shapes=[
            pltpu.VMEM((2, seq, rows), F32),
            pltpu.VMEM((2, 1, rows), F32),
            pltpu.VMEM((2, VT_ROWS, rows), F32),
            pltpu.VMEM((N_HEADS * V_HEAD, rows), BF16),
        ],
        compiler_params=_params(2),
        name="mla_attn",
    )(qt, k, vt, x, mod, g_pair, w_uvt, w_ot)


def kernel(x, c, ada_w, ada_b, norm_g, ffn_w_in, ffn_w_out, pool_w, pool_b, pool_scale,
           mla_w_in, mla_q_norm, mla_kv_norm, mla_w_uq, mla_w_uk, mla_w_uv, mla_w_o):
    batch, seq, _ = x.shape
    mods = _ada_mod(c, ada_w, ada_b).reshape(DEPTH, batch, 3, 3, D_MODEL)
    tables = _rope_tables(seq)
    for i in range(DEPTH):
        mod = mods[i]
        g = norm_g[i]
        li = i // 2
        x = _ffn_sublayer(x, mod, 0, g[0:2], *_ffn_weights(ffn_w_in[i, 0], ffn_w_out[i, 0]))
        if i % 2 == 0:
            x = _pool_sublayer(x, mod, 1, g[2:4], pool_w[li], pool_b[li], pool_scale[li])
        else:
            w_lat, wqt, w_uvt, w_ot = _mla_weights(
                mla_w_in[li], mla_w_uq[li], mla_w_uk[li], mla_w_uv[li], mla_w_o[li])
            qt, k, vt = _mla_pre(x, mod, 1, g[2:4], w_lat, mla_q_norm[li], mla_kv_norm[li],
                                 wqt, tables)
            x = _mla_attn(x, qt, k, vt, mod, 1, g[2:4], w_uvt, w_ot)
        x = _ffn_sublayer(x, mod, 2, g[4:6], *_ffn_weights(ffn_w_in[i, 1], ffn_w_out[i, 1]))
    return x
```

```python
import functools

import jax
import jax.numpy as jnp
import numpy as np
from jax import lax
from jax.experimental import pallas as pl
from jax.experimental.pallas import tpu as pltpu

F32 = jnp.float32
BF16 = jnp.bfloat16

D_MODEL = 1024
DEPTH = 2
N_POOL_GROUPS = 4
POOL_GROUP = D_MODEL // N_POOL_GROUPS
POOL_WINDOWS = (2, 4, 8, 16)
N_HEADS = 16
QK_NOPE = 64
QK_ROPE = 32
V_HEAD = 64
Q_LORA = D_MODEL // 4
KV_LORA = D_MODEL // 8
ROPE_THETA = 10000.0
D_FF = 11 * D_MODEL // 4
EPS = 1e-6
N_MOD = 9
ATTN_SCALE = (QK_NOPE + QK_ROPE) ** -0.5
LOG2_E = 1.4426950408889634
Q_SCALE = ATTN_SCALE * LOG2_E

LANES = 128
SUBLANES = 8
MXU_DIM = 256
VMEM_LIMIT_BYTES = 56 * 1024 * 1024

FFN_CHUNK = MXU_DIM
N_FFN_CHUNKS = D_FF // FFN_CHUNK
FFN_ROWS = 1024
FFN_SUB_ROWS = 256
POOL_ROWS = 512
POOL_HALO = 2 * SUBLANES
POOL_BLOCK = 128
POOL_BAND_K = POOL_BLOCK + 2 * POOL_HALO
MLA_PRE_ROWS = 512
ATTN_ROWS = 512
KEY_TILE = MXU_DIM
SCORE_LEAD = 2
PROJ_HEADS = MXU_DIM // V_HEAD
LAST_HEAD_PROJ_AT = (3, 8, 13)
QK_WIDTH = 2 * LANES
HALF_ROPE = QK_ROPE // 2
QK_FEATS = KV_LORA + QK_ROPE
BF16_SUBLANES = 2 * SUBLANES
VT_ROWS = KV_LORA + BF16_SUBLANES


def _params(n_axes, flags=None):
    return pltpu.CompilerParams(
        dimension_semantics=("arbitrary",) * n_axes,
        vmem_limit_bytes=VMEM_LIMIT_BYTES,
        flags=flags,
    )


def _resident(block_shape, index_map):
    return pl.BlockSpec(block_shape, index_map, pipeline_mode=pl.Buffered(1))


def _rms(v):
    return v * lax.rsqrt(jnp.mean(v * v, axis=-1, keepdims=True) + EPS)


def _modulated_norm(x, mod, g_pre):
    return _rms(x) * (g_pre * (1.0 + mod[1:2])) + mod[0:1]


def _residual(x, y, mod, g_post, weight):
    return x + (weight * (1.0 + mod[2:3])) * (_rms(y) * g_post)


ADA_COLS = 1152


def _ada_kernel(c_ref, w_ref, b_ref, o_ref):
    c = c_ref[...]
    sc = c * (1.0 / (1.0 + jnp.exp(-c)))
    o_ref[...] = jnp.dot(sc.astype(BF16), w_ref[...].astype(BF16),
                         preferred_element_type=F32) + b_ref[...]


def _ada_mod(c, ada_w, ada_b):
    batch = c.shape[0]
    n_out = N_MOD * D_MODEL
    return pl.pallas_call(
        _ada_kernel,
        grid=(DEPTH, n_out // ADA_COLS),
        in_specs=[
            pl.BlockSpec((batch, D_MODEL), lambda l, j: (0, 0)),
            pl.BlockSpec((None, D_MODEL, ADA_COLS), lambda l, j: (l, 0, j)),
            pl.BlockSpec((None, 1, ADA_COLS), lambda l, j: (l, 0, j)),
        ],
        out_specs=pl.BlockSpec((None, batch, ADA_COLS), lambda l, j: (l, 0, j)),
        out_shape=jax.ShapeDtypeStruct((DEPTH, batch, n_out), F32),
        compiler_params=_params(2),
        name="ada_mod",
    )(c, ada_w, ada_b.reshape(DEPTH, 1, n_out))


def _ffn_kernel(x_ref, mod_ref, g_ref, win_ref, wout_ref, o_ref, act_ref):
    mod = mod_ref[...]
    g = g_ref[...]
    n_sub = x_ref.shape[0] // FFN_SUB_ROWS

    def rows_of(s):
        return pl.ds(s * FFN_SUB_ROWS, FFN_SUB_ROWS)

    def normed(s):
        return _modulated_norm(x_ref[rows_of(s), :], mod, g[0:1]).astype(BF16)

    def finish(s):
        o_ref[rows_of(s), :] = _residual(x_ref[rows_of(s), :], o_ref[rows_of(s), :], mod,
                                         g[1:2], 0.5)

    hb = normed(0)
    for s in range(n_sub):
        hb_next = None
        for c in range(N_FFN_CHUNKS):
            gu = jnp.dot(hb, win_ref[:, 2 * c * FFN_CHUNK:2 * (c + 1) * FFN_CHUNK],
                         preferred_element_type=F32)
            gate = gu[:, :FFN_CHUNK]
            up = gu[:, FFN_CHUNK:]
            act = gate * (1.0 / (1.0 + jnp.exp(-gate))) * up
            act_ref[s % 2, :, c * FFN_CHUNK:(c + 1) * FFN_CHUNK] = act.astype(BF16)
            if c == 3 and s + 1 < n_sub:
                hb_next = normed(s + 1)
            if c == 7 and s >= 1:
                finish(s - 1)
        o_ref[rows_of(s), :] = jnp.dot(act_ref[s % 2], wout_ref[...], preferred_element_type=F32)
        hb = hb_next
    finish(n_sub - 1)


def _ffn_sublayer(x, mod, sub, g_pair, w_in_all, w_out_all, layer, which):
    batch, seq, _ = x.shape
    return pl.pallas_call(
        _ffn_kernel,
        grid=(batch, seq // FFN_ROWS),
        in_specs=[
            pl.BlockSpec((None, FFN_ROWS, D_MODEL), lambda b, i: (b, i, 0)),
            pl.BlockSpec((None, None, 3, D_MODEL), lambda b, i: (b, sub, 0, 0)),
            pl.BlockSpec((2, D_MODEL), lambda b, i: (0, 0)),
            _resident((None, None, D_MODEL, 2 * D_FF), lambda b, i: (layer, which, 0, 0)),
            _resident((None, None, D_FF, D_MODEL), lambda b, i: (layer, which, 0, 0)),
        ],
        out_specs=pl.BlockSpec((None, FFN_ROWS, D_MODEL), lambda b, i: (b, i, 0)),
        out_shape=jax.ShapeDtypeStruct(x.shape, F32),
        scratch_shapes=[pltpu.VMEM((2, FFN_SUB_ROWS, D_FF), BF16)],
        compiler_params=_params(2),
        name="ffn_sublayer",
    )(x, mod, g_pair, w_in_all, w_out_all)


def _w_in_prep_kernel(gate_ref, up_ref, o_ref):
    o_ref[:, :FFN_CHUNK] = gate_ref[...].astype(BF16)
    o_ref[:, FFN_CHUNK:] = up_ref[...].astype(BF16)


def _ffn_w_in_prep(ffn_w_in):
    chunk_spec = lambda offset: pl.BlockSpec(
        (None, None, D_MODEL, FFN_CHUNK), lambda l, j, c: (l, j, 0, offset + c))
    return pl.pallas_call(
        _w_in_prep_kernel,
        grid=(DEPTH, 2, N_FFN_CHUNKS),
        in_specs=[chunk_spec(0), chunk_spec(N_FFN_CHUNKS)],
        out_specs=pl.BlockSpec((None, None, D_MODEL, 2 * FFN_CHUNK), lambda l, j, c: (l, j, 0, c)),
        out_shape=jax.ShapeDtypeStruct(ffn_w_in.shape, BF16),
        compiler_params=_params(3),
        name="ffn_w_in_prep",
    )(ffn_w_in, ffn_w_in)


def _pool_bands():
    r = np.arange(POOL_BLOCK)[:, None]
    c = np.arange(POOL_BAND_K)[None, :]
    bands = [(c >= r + POOL_HALO - w // 2) & (c < r + POOL_HALO + w // 2) for w in POOL_WINDOWS]
    return jnp.asarray(np.stack(bands).astype(np.float32), dtype=BF16)


def _pool_edge_ratios(seq):
    ratios = np.ones((2, POOL_HALO, D_MODEL))
    for gi, w in enumerate(POOL_WINDOWS):
        lanes = slice(gi * POOL_GROUP, (gi + 1) * POOL_GROUP)
        for j in range(POOL_HALO):
            for side, t in enumerate((j, seq - POOL_HALO + j)):
                count = min(t + w // 2, seq) - max(t - w // 2, 0)
                ratios[side, j, lanes] = w / count
    return jnp.asarray(ratios.astype(np.float32))


def _pool_kernel(x_ref, xp_ref, xn_ref, mod_ref, g_ref, band_ref, edge_ref, w_ref, b_ref, s_ref,
                 o_ref, hi_ref, lo_ref, mean_ref):
    i = pl.program_id(1)
    n_i = pl.num_programs(1)
    x = x_ref[...]
    mod = mod_ref[...]
    g = g_ref[...]
    rows = x.shape[0]
    h = _modulated_norm(x, mod, g[0:1])

    def store_split(first_row, v):
        hi = v.astype(BF16)
        hi_ref[first_row:first_row + v.shape[0], :] = hi
        lo_ref[first_row:first_row + v.shape[0], :] = (v - hi.astype(F32)).astype(BF16)

    store_split(0, _modulated_norm(xp_ref[...], mod, g[0:1]) * (i > 0).astype(F32))
    store_split(POOL_HALO, h)
    store_split(POOL_HALO + rows, _modulated_norm(xn_ref[...], mod, g[0:1])
                * (i < n_i - 1).astype(F32))

    for gi, window in enumerate(POOL_WINDOWS):
        lanes = slice(gi * POOL_GROUP, (gi + 1) * POOL_GROUP)
        band = band_ref[gi]
        for blk in range(rows // POOL_BLOCK):
            src = pl.ds(blk * POOL_BLOCK, POOL_BAND_K)
            tot = (jnp.dot(band, hi_ref[src, lanes], preferred_element_type=F32)
                   + jnp.dot(band, lo_ref[src, lanes], preferred_element_type=F32))
            mean_ref[blk * POOL_BLOCK:(blk + 1) * POOL_BLOCK, lanes] = tot * (1.0 / window)

    @pl.when(i == 0)
    def _():
        mean_ref[0:POOL_HALO, :] = mean_ref[0:POOL_HALO, :] * edge_ref[0]

    @pl.when(i == n_i - 1)
    def _():
        mean_ref[rows - POOL_HALO:rows, :] = mean_ref[rows - POOL_HALO:rows, :] * edge_ref[1]

    diff = (mean_ref[...] - h).astype(BF16)
    ys = [jnp.dot(diff[:, gi * POOL_GROUP:(gi + 1) * POOL_GROUP], w_ref[gi],
                  preferred_element_type=F32) for gi in range(N_POOL_GROUPS)]
    y = (jnp.concatenate(ys, axis=-1) + b_ref[...]) * s_ref[...]
    o_ref[...] = _residual(x, y, mod, g[1:2], 1.0)


def _pool_sublayer(x, mod, sub, g_pair, pool_w, pool_b, pool_scale):
    batch, seq, _ = x.shape
    halo_blocks = POOL_ROWS // POOL_HALO
    n_halo = seq // POOL_HALO
    return pl.pallas_call(
        _pool_kernel,
        grid=(batch, seq // POOL_ROWS),
        in_specs=[
            pl.BlockSpec((None, POOL_ROWS, D_MODEL), lambda b, i: (b, i, 0)),
            pl.BlockSpec((None, POOL_HALO, D_MODEL),
                         lambda b, i: (b, jnp.maximum(i * halo_blocks - 1, 0), 0)),
            pl.BlockSpec((None, POOL_HALO, D_MODEL),
                         lambda b, i: (b, jnp.minimum((i + 1) * halo_blocks, n_halo - 1), 0)),
            pl.BlockSpec((None, None, 3, D_MODEL), lambda b, i: (b, sub, 0, 0)),
            pl.BlockSpec((2, D_MODEL), lambda b, i: (0, 0)),
            pl.BlockSpec((N_POOL_GROUPS, POOL_BLOCK, POOL_BAND_K), lambda b, i: (0, 0, 0)),
            pl.BlockSpec((2, POOL_HALO, D_MODEL), lambda b, i: (0, 0, 0)),
            pl.BlockSpec((N_POOL_GROUPS, POOL_GROUP, POOL_GROUP), lambda b, i: (0, 0, 0)),
            pl.BlockSpec((1, D_MODEL), lambda b, i: (0, 0)),
            pl.BlockSpec((1, D_MODEL), lambda b, i: (0, 0)),
        ],
        out_specs=pl.BlockSpec((None, POOL_ROWS, D_MODEL), lambda b, i: (b, i, 0)),
        out_shape=jax.ShapeDtypeStruct(x.shape, F32),
        scratch_shapes=[
            pltpu.VMEM((POOL_ROWS + 2 * POOL_HALO, D_MODEL), BF16),
            pltpu.VMEM((POOL_ROWS + 2 * POOL_HALO, D_MODEL), BF16),
            pltpu.VMEM((POOL_ROWS, D_MODEL), F32),
        ],
        compiler_params=_params(2),
        name="pool_sublayer",
    )(x, x, x, mod, g_pair, _pool_bands(), _pool_edge_ratios(seq), pool_w.astype(BF16),
      pool_b.reshape(1, D_MODEL), pool_scale.reshape(1, D_MODEL))


LAT_COLS = Q_LORA + KV_LORA + 2 * LANES


def _fold_qk_kernel(wuq_ref, wuk_ref, o_ref):
    o_ref[...] = Q_SCALE * lax.dot_general(
        wuq_ref[:, :QK_NOPE], wuk_ref[...], (((1,), (1,)), ((), ())),
        preferred_element_type=F32, precision=lax.Precision.HIGHEST)


def _mla_weights(w_in, w_uq, w_uk, w_uv, w_o):
    wuq_h = w_uq.transpose(1, 0, 2)
    wuk_h = w_uk.transpose(1, 0, 2)
    w_ql = pl.pallas_call(
        _fold_qk_kernel,
        grid=(N_HEADS,),
        in_specs=[
            pl.BlockSpec((None, Q_LORA, QK_NOPE + QK_ROPE), lambda h: (h, 0, 0)),
            pl.BlockSpec((None, KV_LORA, QK_NOPE), lambda h: (h, 0, 0)),
        ],
        out_specs=pl.BlockSpec((None, Q_LORA, KV_LORA), lambda h: (h, 0, 0)),
        out_shape=jax.ShapeDtypeStruct((N_HEADS, Q_LORA, KV_LORA), F32),
        compiler_params=_params(1),
        name="fold_qk",
    )(wuq_h, wuk_h)

    def rope_cols(w, swap):
        a, b = w[..., :HALF_ROPE], w[..., HALF_ROPE:]
        first, second = (b, a) if swap else (a, b)
        pad = jnp.zeros(w.shape[:-1] + (LANES - QK_ROPE,), w.dtype)
        return jnp.concatenate([first, second, pad], axis=-1)

    wqt = jnp.concatenate([w_ql, wuq_h[..., QK_NOPE:]], axis=-1).transpose(0, 2, 1).astype(BF16)
    wk_rope = w_in[:, Q_LORA + KV_LORA:]
    w_lat = jnp.concatenate(
        [w_in[:, :Q_LORA + KV_LORA], rope_cols(wk_rope, False), rope_cols(wk_rope, True)],
        axis=-1).astype(BF16)
    w_uvt = w_uv.transpose(1, 2, 0).astype(BF16)
    return w_lat, wqt, w_uvt, w_o.astype(BF16)


def _rope_tables(seq):
    inv = 1.0 / (ROPE_THETA ** (np.arange(0, QK_ROPE, 2, dtype=np.float64) / QK_ROPE))
    ang = np.arange(seq, dtype=np.float64)[:, None] * inv[None, :]
    cos, sin = np.cos(ang), np.sin(ang)
    pad = np.zeros((seq, LANES - QK_ROPE))
    as_f32 = lambda a: jnp.asarray(a.astype(np.float32))
    return (as_f32(np.concatenate([cos, cos, pad], axis=-1)),
            as_f32(np.concatenate([-sin, sin, pad], axis=-1)),
            as_f32(cos.T * Q_SCALE), as_f32(sin.T * Q_SCALE))


def _mla_pre_kernel(x_ref, mod_ref, g_ref, wlat_ref, qn_ref, kvn_ref, wqt_ref,
                    cosk_ref, sink_ref, cosq_ref, sinq_ref, qt_ref, k_ref, vt_ref):
    x = x_ref[...]
    rows = x.shape[0]
    hb = _modulated_norm(x, mod_ref[...], g_ref[0:1, :]).astype(BF16)
    lat = jnp.dot(hb, wlat_ref[...], preferred_element_type=F32)
    c_q = _rms(lat[:, :Q_LORA]) * qn_ref[...]
    c_kv = _rms(lat[:, Q_LORA:Q_LORA + KV_LORA]) * kvn_ref[...]
    k_rope = (lat[:, Q_LORA + KV_LORA:Q_LORA + KV_LORA + LANES] * cosk_ref[...]
              + lat[:, Q_LORA + KV_LORA + LANES:] * sink_ref[...])
    k_ref[...] = jnp.concatenate([c_kv, k_rope], axis=-1).astype(BF16)
    vt_ref[0:KV_LORA, :] = c_kv.T.astype(BF16)
    ones_row = lax.broadcasted_iota(jnp.int32, (VT_ROWS - KV_LORA, rows), 0) == 0
    vt_ref[KV_LORA:, :] = ones_row.astype(BF16)
    cqt = c_q.T.astype(BF16)
    cos_q = cosq_ref[...]
    sin_q = sinq_ref[...]
    r1 = KV_LORA + HALF_ROPE
    for head in range(N_HEADS):
        a = jnp.dot(wqt_ref[head], cqt, preferred_element_type=F32)
        x1 = a[KV_LORA:r1]
        x2 = a[r1:]
        qt_ref[head, 0:KV_LORA, :] = a[:KV_LORA].astype(BF16)
        qt_ref[head, KV_LORA:r1, :] = (x1 * cos_q - x2 * sin_q).astype(BF16)
        qt_ref[head, r1:QK_FEATS, :] = (x2 * cos_q + x1 * sin_q).astype(BF16)


def _mla_pre(x, mod, sub, g_pair, w_lat, q_norm, kv_norm, wqt, tables):
    batch, seq, _ = x.shape
    rows = MLA_PRE_ROWS
    return pl.pallas_call(
        _mla_pre_kernel,
        grid=(batch, seq // rows),
        in_specs=[
            pl.BlockSpec((None, rows, D_MODEL), lambda b, i: (b, i, 0)),
            pl.BlockSpec((None, None, 3, D_MODEL), lambda b, i: (b, sub, 0, 0)),
            pl.BlockSpec((2, D_MODEL), lambda b, i: (0, 0)),
            pl.BlockSpec((D_MODEL, LAT_COLS), lambda b, i: (0, 0)),
            pl.BlockSpec((1, Q_LORA), lambda b, i: (0, 0)),
            pl.BlockSpec((1, KV_LORA), lambda b, i: (0, 0)),
            pl.BlockSpec((N_HEADS, QK_FEATS, Q_LORA), lambda b, i: (0, 0, 0)),
            pl.BlockSpec((rows, LANES), lambda b, i: (i, 0)),
            pl.BlockSpec((rows, LANES), lambda b, i: (i, 0)),
            pl.BlockSpec((HALF_ROPE, rows), lambda b, i: (0, i)),
            pl.BlockSpec((HALF_ROPE, rows), lambda b, i: (0, i)),
        ],
        out_specs=[
            pl.BlockSpec((None, N_HEADS, QK_FEATS, rows), lambda b, i: (b, 0, 0, i)),
            pl.BlockSpec((None, rows, QK_WIDTH), lambda b, i: (b, i, 0)),
            pl.BlockSpec((None, VT_ROWS, rows), lambda b, i: (b, 0, i)),
        ],
        out_shape=[
            jax.ShapeDtypeStruct((batch, N_HEADS, QK_FEATS, seq), BF16),
            jax.ShapeDtypeStruct((batch, seq, QK_WIDTH), BF16),
            jax.ShapeDtypeStruct((batch, VT_ROWS, seq), BF16),
        ],
        compiler_params=_params(2),
        name="mla_pre",
    )(x, mod, g_pair, w_lat, q_norm.reshape(1, Q_LORA), kv_norm.reshape(1, KV_LORA),
      wqt, *tables)


def _mla_attn_kernel(qt_ref, k_ref, vt_ref, x_ref, mod_ref, g_ref, wuvt_ref, wo_ref, o_ref,
                     st_ref, m_ref, acc_ref, ov_ref):
    seq = k_ref.shape[0]
    n_tiles = seq // KEY_TILE

    def score_tile(qt_next, slot, kt, mx):
        keys = pl.ds(kt * KEY_TILE, KEY_TILE)
        st = jnp.dot(k_ref[keys, 0:QK_FEATS], qt_next, preferred_element_type=F32)
        st_ref[slot, keys, :] = st
        tmax = jnp.max(st.reshape(KEY_TILE // SUBLANES, SUBLANES, st.shape[1]), axis=0)
        return tmax if mx is None else jnp.maximum(mx, tmax)

    def prob_tile(slot, kt, m, acc):
        keys = pl.ds(kt * KEY_TILE, KEY_TILE)
        pt = jnp.exp2(st_ref[slot, keys, :] - m).astype(BF16)
        part = jnp.dot(vt_ref[:, keys], pt, preferred_element_type=F32)
        return part if acc is None else acc + part

    def scores(head, slot):
        qt_next = qt_ref[head]
        mx = None
        for kt in range(n_tiles):
            mx = score_tile(qt_next, slot, kt, mx)
        m_ref[slot] = jnp.max(mx, axis=0, keepdims=True)

    def emit(head, slot):
        acc = acc_ref[slot]
        ot = acc[:KV_LORA] * (1.0 / acc[KV_LORA:KV_LORA + 1])
        ovt = jnp.dot(wuvt_ref[head], ot.astype(BF16), preferred_element_type=F32)
        ov_ref[pl.ds(pl.multiple_of(head * V_HEAD, V_HEAD), V_HEAD), :] = ovt.astype(BF16)

    def out_proj_chunk(c, y):
        feats = pl.ds(c * PROJ_HEADS * V_HEAD, PROJ_HEADS * V_HEAD)
        part = lax.dot_general(ov_ref[feats, :], wo_ref[feats, :], (((0,), (0,)), ((), ())),
                               preferred_element_type=F32)
        return part if y is None else y + part

    def overlapped(head, slot, emit_previous=True, score_next=True, proj_at=()):
        qt_next = qt_ref[head + 1] if score_next else None
        m = m_ref[slot]
        acc = None
        mx = None
        y = None
        for kt in range(n_tiles + SCORE_LEAD):
            if score_next and kt < n_tiles:
                mx = score_tile(qt_next, 1 - slot, kt, mx)
            if emit_previous and kt == SCORE_LEAD - 1:
                emit(head - 1, 1 - slot)
            if kt >= SCORE_LEAD:
                acc = prob_tile(slot, kt - SCORE_LEAD, m, acc)
            if kt in proj_at:
                y = out_proj_chunk(proj_at.index(kt), y)
        if score_next:
            m_ref[1 - slot] = jnp.max(mx, axis=0, keepdims=True)
        acc_ref[slot] = acc
        return y

    def pair_body(j, carry):
        overlapped(2 * j + 1, 1)
        overlapped(2 * j + 2, 0)
        return carry

    scores(0, 0)
    overlapped(0, 0, emit_previous=False)
    lax.fori_loop(0, N_HEADS // 2 - 1, pair_body, 0)
    n_proj = N_HEADS // PROJ_HEADS
    y = overlapped(N_HEADS - 1, 1, score_next=False, proj_at=LAST_HEAD_PROJ_AT[:n_proj - 1])
    emit(N_HEADS - 1, 1)
    y = out_proj_chunk(n_proj - 1, y)
    o_ref[...] = _residual(x_ref[...], y, mod_ref[...], g_ref[1:2, :], 1.0)


def _mla_attn(x, qt, k, vt, mod, sub, g_pair, w_uvt, w_ot):
    batch, seq, _ = x.shape
    rows = ATTN_ROWS
    return pl.pallas_call(
        _mla_attn_kernel,
        grid=(batch, seq // rows),
        in_specs=[
            pl.BlockSpec((None, N_HEADS, QK_FEATS, rows), lambda b, i: (b, 0, 0, i)),
            pl.BlockSpec((None, seq, QK_WIDTH), lambda b, i: (b, 0, 0)),
            pl.BlockSpec((None, VT_ROWS, seq), lambda b, i: (b, 0, 0)),
            pl.BlockSpec((None, rows, D_MODEL), lambda b, i: (b, i, 0)),
            pl.BlockSpec((None, None, 3, D_MODEL), lambda b, i: (b, sub, 0, 0)),
            pl.BlockSpec((2, D_MODEL), lambda b, i: (0, 0)),
            _resident((N_HEADS, V_HEAD, KV_LORA), lambda b, i: (0, 0, 0)),
            _resident((D_MODEL, N_HEADS * V_HEAD), lambda b, i: (0, 0)),
        ],
        out_specs=pl.BlockSpec((None, rows, D_MODEL), lambda b, i: (b, i, 0)),
        out_shape=jax.ShapeDtypeStruct(x.shape, F32),
        scratch_shapes=[
            pltpu.VMEM((2, seq, rows), F32),
            pltpu.VMEM((2, 1, rows), F32),
            pltpu.VMEM((2, VT_ROWS, rows), F32),
            pltpu.VMEM((N_HEADS * V_HEAD, rows), BF16),
        ],
        compiler_params=_params(2),
        name="mla_attn",
    )(qt, k, vt, x, mod, g_pair, w_uvt, w_ot)


def kernel(x, c, ada_w, ada_b, norm_g, ffn_w_in, ffn_w_out, pool_w, pool_b, pool_scale,
           mla_w_in, mla_q_norm, mla_kv_norm, mla_w_uq, mla_w_uk, mla_w_uv, mla_w_o):
    batch, seq, _ = x.shape
    mods = _ada_mod(c, ada_w, ada_b).reshape(DEPTH, batch, 3, 3, D_MODEL)
    tables = _rope_tables(seq)
    w_in_all = _ffn_w_in_prep(ffn_w_in)
    w_out_all = ffn_w_out.astype(BF16)
    for i in range(DEPTH):
        mod = mods[i]
        g = norm_g[i]
        li = i // 2
        x = _ffn_sublayer(x, mod, 0, g[0:2], w_in_all, w_out_all, i, 0)
        if i % 2 == 0:
            x = _pool_sublayer(x, mod, 1, g[2:4], pool_w[li], pool_b[li], pool_scale[li])
        else:
            w_lat, wqt, w_uvt, w_ot = _mla_weights(
                mla_w_in[li], mla_w_uq[li], mla_w_uk[li], mla_w_uv[li], mla_w_o[li])
            qt, k, vt = _mla_pre(x, mod, 1, g[2:4], w_lat, mla_q_norm[li], mla_kv_norm[li],
                                 wqt, tables)
            x = _mla_attn(x, qt, k, vt, mod, 1, g[2:4], w_uvt, w_ot)
        x = _ffn_sublayer(x, mod, 2, g[4:6], w_in_all, w_out_all, i, 1)
    return x
```

```python
import jax
import jax.numpy as jnp
import numpy as np
from jax import lax
from jax.experimental import pallas as pl
from jax.experimental.pallas import tpu as pltpu

F32 = jnp.float32
BF16 = jnp.bfloat16

D_MODEL = 1024
DEPTH = 2
N_POOL_GROUPS = 4
POOL_GROUP = D_MODEL // N_POOL_GROUPS
POOL_WINDOWS = (2, 4, 8, 16)
N_HEADS = 16
QK_NOPE = 64
QK_ROPE = 32
V_HEAD = 64
Q_LORA = D_MODEL // 4
KV_LORA = D_MODEL // 8
ROPE_THETA = 10000.0
D_FF = 11 * D_MODEL // 4
EPS = 1e-6
N_MOD = 9
ATTN_SCALE = (QK_NOPE + QK_ROPE) ** -0.5
LOG2_E = 1.4426950408889634
Q_SCALE = ATTN_SCALE * LOG2_E

LANES = 128
SUBLANES = 8
MXU_DIM = 256
VMEM_LIMIT_BYTES = 56 * 1024 * 1024

FFN_CHUNK = MXU_DIM
N_FFN_CHUNKS = D_FF // FFN_CHUNK
FFN_ROWS = 1024
FFN_SUB_ROWS = 256
POOL_ROWS = 1024
POOL_HALO = 2 * SUBLANES
POOL_BLOCK = 128
POOL_BAND_K = POOL_BLOCK + 2 * POOL_HALO
MLA_PRE_ROWS = 1024
ATTN_ROWS = 512
KEY_TILE = MXU_DIM
SCORE_LEAD = 2
EMIT_AT = SCORE_LEAD - 1
QK_WIDTH = 2 * LANES
HALF_ROPE = QK_ROPE // 2
QK_FEATS = KV_LORA + QK_ROPE
BF16_SUBLANES = 2 * SUBLANES
VT_ROWS = KV_LORA + BF16_SUBLANES


def _params(n_axes, flags=None):
    return pltpu.CompilerParams(
        dimension_semantics=("arbitrary",) * n_axes,
        vmem_limit_bytes=VMEM_LIMIT_BYTES,
        flags=flags,
    )


def _resident(block_shape, index_map):
    return pl.BlockSpec(block_shape, index_map, pipeline_mode=pl.Buffered(1))


def _rms(v):
    return v * lax.rsqrt(jnp.mean(v * v, axis=-1, keepdims=True) + EPS)


def _modulated_norm(x, mod, g_pre):
    return _rms(x) * (g_pre * (1.0 + mod[1:2])) + mod[0:1]


def _residual(x, y, mod, g_post, weight):
    return x + (weight * (1.0 + mod[2:3])) * (_rms(y) * g_post)


ADA_COLS = 1152


def _ada_kernel(c_ref, w_ref, b_ref, o_ref):
    c = c_ref[...]
    sc = c * (1.0 / (1.0 + jnp.exp(-c)))
    o_ref[...] = jnp.dot(sc.astype(BF16), w_ref[...].astype(BF16),
                         preferred_element_type=F32) + b_ref[...]


def _ada_mod(c, ada_w, ada_b):
    batch = c.shape[0]
    n_out = N_MOD * D_MODEL
    return pl.pallas_call(
        _ada_kernel,
        grid=(DEPTH, n_out // ADA_COLS),
        in_specs=[
            pl.BlockSpec((batch, D_MODEL), lambda l, j: (0, 0)),
            pl.BlockSpec((None, D_MODEL, ADA_COLS), lambda l, j: (l, 0, j)),
            pl.BlockSpec((None, 1, ADA_COLS), lambda l, j: (l, 0, j)),
        ],
        out_specs=pl.BlockSpec((None, batch, ADA_COLS), lambda l, j: (l, 0, j)),
        out_shape=jax.ShapeDtypeStruct((DEPTH, batch, n_out), F32),
        compiler_params=_params(2),
        name="ada_mod",
    )(c, ada_w, ada_b.reshape(DEPTH, 1, n_out))


def _ffn_kernel(x_ref, mod_ref, g_ref, win_ref, wout_ref, o_ref, act_ref):
    mod = mod_ref[...]
    g = g_ref[...]
    n_sub = x_ref.shape[0] // FFN_SUB_ROWS

    def rows_of(s):
        return pl.ds(s * FFN_SUB_ROWS, FFN_SUB_ROWS)

    def normed(s, after=None):
        x = x_ref[rows_of(s), :]
        ms = jnp.mean(x * x, axis=-1, keepdims=True) + EPS
        if after is not None:
            bits = pltpu.bitcast(after, jnp.uint32)
            ms = ms + pltpu.bitcast((bits >> 16) >> 16, F32)
        return ((x * lax.rsqrt(ms)) * (g[0:1] * (1.0 + mod[1:2])) + mod[0:1]).astype(BF16)

    def finish(s):
        o_ref[rows_of(s), :] = _residual(x_ref[rows_of(s), :], o_ref[rows_of(s), :], mod,
                                         g[1:2], 0.5)

    hb = normed(0)
    for s in range(n_sub):
        hb_next = None
        for c in range(N_FFN_CHUNKS):
            gu = jnp.dot(hb, win_ref[:, 2 * c * FFN_CHUNK:2 * (c + 1) * FFN_CHUNK],
                         preferred_element_type=F32)
            gate = gu[:, :FFN_CHUNK]
            up = gu[:, FFN_CHUNK:]
            act = gate * (1.0 / (1.0 + jnp.exp(-gate))) * up
            act_ref[s % 2, :, c * FFN_CHUNK:(c + 1) * FFN_CHUNK] = act.astype(BF16)
            if c == 3 and s + 1 < n_sub:
                hb_next = normed(s + 1, after=gu[:, 0:1])
            if c == 7 and s >= 1:
                finish(s - 1)
        o_ref[rows_of(s), :] = jnp.dot(act_ref[s % 2], wout_ref[...], preferred_element_type=F32)
        hb = hb_next
    finish(n_sub - 1)


def _ffn_sublayer(x, mod, sub, g_pair, w_in_all, w_out_all, layer, which):
    batch, seq, _ = x.shape
    return pl.pallas_call(
        _ffn_kernel,
        grid=(batch, seq // FFN_ROWS),
        in_specs=[
            pl.BlockSpec((None, FFN_ROWS, D_MODEL), lambda b, i: (b, i, 0)),
            pl.BlockSpec((None, None, 3, D_MODEL), lambda b, i: (b, sub, 0, 0)),
            pl.BlockSpec((2, D_MODEL), lambda b, i: (0, 0)),
            _resident((None, None, D_MODEL, 2 * D_FF), lambda b, i: (layer, which, 0, 0)),
            _resident((None, None, D_FF, D_MODEL), lambda b, i: (layer, which, 0, 0)),
        ],
        out_specs=pl.BlockSpec((None, FFN_ROWS, D_MODEL), lambda b, i: (b, i, 0)),
        out_shape=jax.ShapeDtypeStruct(x.shape, F32),
        scratch_shapes=[pltpu.VMEM((2, FFN_SUB_ROWS, D_FF), BF16)],
        compiler_params=_params(2),
        name="ffn_sublayer",
    )(x, mod, g_pair, w_in_all, w_out_all)


def _w_in_prep_kernel(gate_ref, up_ref, o_ref):
    o_ref[:, :FFN_CHUNK] = gate_ref[...].astype(BF16)
    o_ref[:, FFN_CHUNK:] = up_ref[...].astype(BF16)


def _ffn_w_in_prep(ffn_w_in):
    chunk_spec = lambda offset: pl.BlockSpec(
        (None, None, D_MODEL, FFN_CHUNK), lambda l, j, c: (l, j, 0, offset + c))
    return pl.pallas_call(
        _w_in_prep_kernel,
        grid=(DEPTH, 2, N_FFN_CHUNKS),
        in_specs=[chunk_spec(0), chunk_spec(N_FFN_CHUNKS)],
        out_specs=pl.BlockSpec((None, None, D_MODEL, 2 * FFN_CHUNK), lambda l, j, c: (l, j, 0, c)),
        out_shape=jax.ShapeDtypeStruct(ffn_w_in.shape, BF16),
        compiler_params=_params(3),
        name="ffn_w_in_prep",
    )(ffn_w_in, ffn_w_in)


def _pool_bands():
    r = np.arange(POOL_BLOCK)[:, None]
    c = np.arange(POOL_BAND_K)[None, :]
    bands = [(c >= r + POOL_HALO - w // 2) & (c < r + POOL_HALO + w // 2) for w in POOL_WINDOWS]
    return jnp.asarray(np.stack(bands).astype(np.float32), dtype=BF16)


def _pool_edge_ratios(seq):
    ratios = np.ones((2, POOL_HALO, D_MODEL))
    for gi, w in enumerate(POOL_WINDOWS):
        lanes = slice(gi * POOL_GROUP, (gi + 1) * POOL_GROUP)
        for j in range(POOL_HALO):
            for side, t in enumerate((j, seq - POOL_HALO + j)):
                count = min(t + w // 2, seq) - max(t - w // 2, 0)
                ratios[side, j, lanes] = w / count
    return jnp.asarray(ratios.astype(np.float32))


def _pool_kernel(x_ref, xp_ref, xn_ref, mod_ref, g_ref, band_ref, edge_ref, w_ref, b_ref, s_ref,
                 o_ref, hi_ref, lo_ref, mean_ref):
    i = pl.program_id(1)
    n_i = pl.num_programs(1)
    x = x_ref[...]
    mod = mod_ref[...]
    g = g_ref[...]
    rows = x.shape[0]
    h = _modulated_norm(x, mod, g[0:1])

    def store_split(first_row, v):
        hi = v.astype(BF16)
        hi_ref[first_row:first_row + v.shape[0], :] = hi
        lo_ref[first_row:first_row + v.shape[0], :] = (v - hi.astype(F32)).astype(BF16)

    store_split(0, _modulated_norm(xp_ref[...], mod, g[0:1]) * (i > 0).astype(F32))
    store_split(POOL_HALO, h)
    store_split(POOL_HALO + rows, _modulated_norm(xn_ref[...], mod, g[0:1])
                * (i < n_i - 1).astype(F32))

    for gi, window in enumerate(POOL_WINDOWS):
        lanes = slice(gi * POOL_GROUP, (gi + 1) * POOL_GROUP)
        band = band_ref[gi]
        for blk in range(rows // POOL_BLOCK):
            src = pl.ds(blk * POOL_BLOCK, POOL_BAND_K)
            tot = (jnp.dot(band, hi_ref[src, lanes], preferred_element_type=F32)
                   + jnp.dot(band, lo_ref[src, lanes], preferred_element_type=F32))
            mean_ref[blk * POOL_BLOCK:(blk + 1) * POOL_BLOCK, lanes] = tot * (1.0 / window)

    @pl.when(i == 0)
    def _():
        mean_ref[0:POOL_HALO, :] = mean_ref[0:POOL_HALO, :] * edge_ref[0]

    @pl.when(i == n_i - 1)
    def _():
        mean_ref[rows - POOL_HALO:rows, :] = mean_ref[rows - POOL_HALO:rows, :] * edge_ref[1]

    diff = (mean_ref[...] - h).astype(BF16)
    ys = [jnp.dot(diff[:, gi * POOL_GROUP:(gi + 1) * POOL_GROUP], w_ref[gi],
                  preferred_element_type=F32) for gi in range(N_POOL_GROUPS)]
    y = (jnp.concatenate(ys, axis=-1) + b_ref[...]) * s_ref[...]
    o_ref[...] = _residual(x, y, mod, g[1:2], 1.0)


def _pool_sublayer(x, mod, sub, g_pair, pool_w, pool_b, pool_scale):
    batch, seq, _ = x.shape
    halo_blocks = POOL_ROWS // POOL_HALO
    n_halo = seq // POOL_HALO
    return pl.pallas_call(
        _pool_kernel,
        grid=(batch, seq // POOL_ROWS),
        in_specs=[
            pl.BlockSpec((None, POOL_ROWS, D_MODEL), lambda b, i: (b, i, 0)),
            pl.BlockSpec((None, POOL_HALO, D_MODEL),
                         lambda b, i: (b, jnp.maximum(i * halo_blocks - 1, 0), 0)),
            pl.BlockSpec((None, POOL_HALO, D_MODEL),
                         lambda b, i: (b, jnp.minimum((i + 1) * halo_blocks, n_halo - 1), 0)),
            pl.BlockSpec((None, None, 3, D_MODEL), lambda b, i: (b, sub, 0, 0)),
            pl.BlockSpec((2, D_MODEL), lambda b, i: (0, 0)),
            pl.BlockSpec((N_POOL_GROUPS, POOL_BLOCK, POOL_BAND_K), lambda b, i: (0, 0, 0)),
            pl.BlockSpec((2, POOL_HALO, D_MODEL), lambda b, i: (0, 0, 0)),
            pl.BlockSpec((N_POOL_GROUPS, POOL_GROUP, POOL_GROUP), lambda b, i: (0, 0, 0)),
            pl.BlockSpec((1, D_MODEL), lambda b, i: (0, 0)),
            pl.BlockSpec((1, D_MODEL), lambda b, i: (0, 0)),
        ],
        out_specs=pl.BlockSpec((None, POOL_ROWS, D_MODEL), lambda b, i: (b, i, 0)),
        out_shape=jax.ShapeDtypeStruct(x.shape, F32),
        scratch_shapes=[
            pltpu.VMEM((POOL_ROWS + 2 * POOL_HALO, D_MODEL), BF16),
            pltpu.VMEM((POOL_ROWS + 2 * POOL_HALO, D_MODEL), BF16),
            pltpu.VMEM((POOL_ROWS, D_MODEL), F32),
        ],
        compiler_params=_params(2),
        name="pool_sublayer",
    )(x, x, x, mod, g_pair, _pool_bands(), _pool_edge_ratios(seq), pool_w.astype(BF16),
      pool_b.reshape(1, D_MODEL), pool_scale.reshape(1, D_MODEL))


LAT_COLS = Q_LORA + KV_LORA + 2 * LANES


def _fold_qk_kernel(wuq_ref, wuk_ref, o_ref):
    o_ref[...] = Q_SCALE * lax.dot_general(
        wuq_ref[:, :QK_NOPE], wuk_ref[...], (((1,), (1,)), ((), ())),
        preferred_element_type=F32, precision=lax.Precision.HIGHEST)


def _mla_weights(w_in, w_uq, w_uk, w_uv, w_o):
    wuq_h = w_uq.transpose(1, 0, 2)
    wuk_h = w_uk.transpose(1, 0, 2)
    w_ql = pl.pallas_call(
        _fold_qk_kernel,
        grid=(N_HEADS,),
        in_specs=[
            pl.BlockSpec((None, Q_LORA, QK_NOPE + QK_ROPE), lambda h: (h, 0, 0)),
            pl.BlockSpec((None, KV_LORA, QK_NOPE), lambda h: (h, 0, 0)),
        ],
        out_specs=pl.BlockSpec((None, Q_LORA, KV_LORA), lambda h: (h, 0, 0)),
        out_shape=jax.ShapeDtypeStruct((N_HEADS, Q_LORA, KV_LORA), F32),
        compiler_params=_params(1),
        name="fold_qk",
    )(wuq_h, wuk_h)

    def rope_cols(w, swap):
        a, b = w[..., :HALF_ROPE], w[..., HALF_ROPE:]
        first, second = (b, a) if swap else (a, b)
        pad = jnp.zeros(w.shape[:-1] + (LANES - QK_ROPE,), w.dtype)
        return jnp.concatenate([first, second, pad], axis=-1)

    wqt = jnp.concatenate([w_ql, wuq_h[..., QK_NOPE:]], axis=-1).transpose(0, 2, 1).astype(BF16)
    wk_rope = w_in[:, Q_LORA + KV_LORA:]
    w_lat = jnp.concatenate(
        [w_in[:, :Q_LORA + KV_LORA], rope_cols(wk_rope, False), rope_cols(wk_rope, True)],
        axis=-1).astype(BF16)
    w_uvt = w_uv.transpose(1, 2, 0).astype(BF16)
    return w_lat, wqt, w_uvt, w_o.astype(BF16)


def _rope_tables(seq):
    inv = 1.0 / (ROPE_THETA ** (np.arange(0, QK_ROPE, 2, dtype=np.float64) / QK_ROPE))
    ang = np.arange(seq, dtype=np.float64)[:, None] * inv[None, :]
    cos, sin = np.cos(ang), np.sin(ang)
    pad = np.zeros((seq, LANES - QK_ROPE))
    as_f32 = lambda a: jnp.asarray(a.astype(np.float32))
    return (as_f32(np.concatenate([cos, cos, pad], axis=-1)),
            as_f32(np.concatenate([-sin, sin, pad], axis=-1)),
            as_f32(cos.T * Q_SCALE), as_f32(sin.T * Q_SCALE))


def _mla_pre_kernel(x_ref, mod_ref, g_ref, wlat_ref, qn_ref, kvn_ref, wqt_ref,
                    cosk_ref, sink_ref, cosq_ref, sinq_ref, qt_ref, k_ref, vt_ref):
    x = x_ref[...]
    rows = x.shape[0]
    hb = _modulated_norm(x, mod_ref[...], g_ref[0:1, :]).astype(BF16)
    lat = jnp.dot(hb, wlat_ref[...], preferred_element_type=F32)
    c_q = _rms(lat[:, :Q_LORA]) * qn_ref[...]
    c_kv = _rms(lat[:, Q_LORA:Q_LORA + KV_LORA]) * kvn_ref[...]
    k_rope = (lat[:, Q_LORA + KV_LORA:Q_LORA + KV_LORA + LANES] * cosk_ref[...]
              + lat[:, Q_LORA + KV_LORA + LANES:] * sink_ref[...])
    k_ref[...] = jnp.concatenate([c_kv, k_rope], axis=-1).astype(BF16)
    vt_ref[0:KV_LORA, :] = c_kv.T.astype(BF16)
    ones_row = lax.broadcasted_iota(jnp.int32, (VT_ROWS - KV_LORA, rows), 0) == 0
    vt_ref[KV_LORA:, :] = ones_row.astype(BF16)
    cqt = c_q.T.astype(BF16)
    cos_q = cosq_ref[...]
    sin_q = sinq_ref[...]
    r1 = KV_LORA + HALF_ROPE
    for head in range(N_HEADS):
        a = jnp.dot(wqt_ref[head], cqt, preferred_element_type=F32)
        x1 = a[KV_LORA:r1]
        x2 = a[r1:]
        qt_ref[head, 0:KV_LORA, :] = a[:KV_LORA].astype(BF16)
        qt_ref[head, KV_LORA:r1, :] = (x1 * cos_q - x2 * sin_q).astype(BF16)
        qt_ref[head, r1:QK_FEATS, :] = (x2 * cos_q + x1 * sin_q).astype(BF16)


def _mla_pre(x, mod, sub, g_pair, w_lat, q_norm, kv_norm, wqt, tables):
    batch, seq, _ = x.shape
    rows = MLA_PRE_ROWS
    return pl.pallas_call(
        _mla_pre_kernel,
        grid=(batch, seq // rows),
        in_specs=[
            pl.BlockSpec((None, rows, D_MODEL), lambda b, i: (b, i, 0)),
            pl.BlockSpec((None, None, 3, D_MODEL), lambda b, i: (b, sub, 0, 0)),
            pl.BlockSpec((2, D_MODEL), lambda b, i: (0, 0)),
            pl.BlockSpec((D_MODEL, LAT_COLS), lambda b, i: (0, 0)),
            pl.BlockSpec((1, Q_LORA), lambda b, i: (0, 0)),
            pl.BlockSpec((1, KV_LORA), lambda b, i: (0, 0)),
            pl.BlockSpec((N_HEADS, QK_FEATS, Q_LORA), lambda b, i: (0, 0, 0)),
            pl.BlockSpec((rows, LANES), lambda b, i: (i, 0)),
            pl.BlockSpec((rows, LANES), lambda b, i: (i, 0)),
            pl.BlockSpec((HALF_ROPE, rows), lambda b, i: (0, i)),
            pl.BlockSpec((HALF_ROPE, rows), lambda b, i: (0, i)),
        ],
        out_specs=[
            pl.BlockSpec((None, N_HEADS, QK_FEATS, rows), lambda b, i: (b, 0, 0, i)),
            pl.BlockSpec((None, rows, QK_WIDTH), lambda b, i: (b, i, 0)),
            pl.BlockSpec((None, VT_ROWS, rows), lambda b, i: (b, 0, i)),
        ],
        out_shape=[
            jax.ShapeDtypeStruct((batch, N_HEADS, QK_FEATS, seq), BF16),
            jax.ShapeDtypeStruct((batch, seq, QK_WIDTH), BF16),
            jax.ShapeDtypeStruct((batch, VT_ROWS, seq), BF16),
        ],
        compiler_params=_params(2),
        name="mla_pre",
    )(x, mod, g_pair, w_lat, q_norm.reshape(1, Q_LORA), kv_norm.reshape(1, KV_LORA),
      wqt, *tables)


def _mla_attn_kernel(qt_ref, k_ref, vt_ref, x_ref, mod_ref, g_ref, wuvt_ref, wo_ref, o_ref,
                     st_ref, m_ref, acc_ref, ov_ref):
    seq = k_ref.shape[0]
    n_tiles = seq // KEY_TILE

    def score_tile(qt_next, slot, kt, mx):
        keys = pl.ds(kt * KEY_TILE, KEY_TILE)
        st = jnp.dot(k_ref[keys, 0:QK_FEATS], qt_next, preferred_element_type=F32)
        st_ref[slot, keys, :] = st
        tmax = jnp.max(st.reshape(KEY_TILE // SUBLANES, SUBLANES, st.shape[1]), axis=0)
        return tmax if mx is None else jnp.maximum(mx, tmax)

    def prob_tile(slot, kt, m, acc):
        keys = pl.ds(kt * KEY_TILE, KEY_TILE)
        pt = jnp.exp2(st_ref[slot, keys, :] - m).astype(BF16)
        part = jnp.dot(vt_ref[:, keys], pt, preferred_element_type=F32)
        return part if acc is None else acc + part

    def scores(head, slot):
        qt_next = qt_ref[head]
        mx = None
        for kt in range(n_tiles):
            mx = score_tile(qt_next, slot, kt, mx)
        m_ref[slot] = jnp.max(mx, axis=0, keepdims=True)

    def emit(head, slot):
        acc = acc_ref[slot]
        ot = acc[:KV_LORA] * (1.0 / acc[KV_LORA:KV_LORA + 1])
        ovt = jnp.dot(wuvt_ref[head], ot.astype(BF16), preferred_element_type=F32)
        ov_ref[pl.ds(pl.multiple_of(head * V_HEAD, V_HEAD), V_HEAD), :] = ovt.astype(BF16)

    def overlapped(head, slot, emit_previous=True, score_next=True):
        qt_next = qt_ref[head + 1] if score_next else None
        m = m_ref[slot]
        acc = None
        mx = None
        for kt in range(n_tiles + SCORE_LEAD):
            if score_next and kt < n_tiles:
                mx = score_tile(qt_next, 1 - slot, kt, mx)
            if emit_previous and kt == EMIT_AT:
                emit(head - 1, 1 - slot)
            if kt >= SCORE_LEAD:
                acc = prob_tile(slot, kt - SCORE_LEAD, m, acc)
        if score_next:
            m_ref[1 - slot] = jnp.max(mx, axis=0, keepdims=True)
        acc_ref[slot] = acc

    def pair_body(j, carry):
        overlapped(2 * j + 1, 1)
        overlapped(2 * j + 2, 0)
        return carry

    scores(0, 0)
    overlapped(0, 0, emit_previous=False)
    lax.fori_loop(0, N_HEADS // 2 - 1, pair_body, 0)
    overlapped(N_HEADS - 1, 1, score_next=False)
    emit(N_HEADS - 1, 1)
    y = lax.dot_general(ov_ref[...], wo_ref[...], (((0,), (0,)), ((), ())),
                        preferred_element_type=F32)
    o_ref[...] = _residual(x_ref[...], y, mod_ref[...], g_ref[1:2, :], 1.0)


def _mla_attn(x, qt, k, vt, mod, sub, g_pair, w_uvt, w_ot):
    batch, seq, _ = x.shape
    rows = ATTN_ROWS
    return pl.pallas_call(
        _mla_attn_kernel,
        grid=(batch, seq // rows),
        in_specs=[
            pl.BlockSpec((None, N_HEADS, QK_FEATS, rows), lambda b, i: (b, 0, 0, i)),
            pl.BlockSpec((None, seq, QK_WIDTH), lambda b, i: (b, 0, 0)),
            pl.BlockSpec((None, VT_ROWS, seq), lambda b, i: (b, 0, 0)),
            pl.BlockSpec((None, rows, D_MODEL), lambda b, i: (b, i, 0)),
            pl.BlockSpec((None, None, 3, D_MODEL), lambda b, i: (b, sub, 0, 0)),
            pl.BlockSpec((2, D_MODEL), lambda b, i: (0, 0)),
            _resident((N_HEADS, V_HEAD, KV_LORA), lambda b, i: (0, 0, 0)),
            _resident((D_MODEL, N_HEADS * V_HEAD), lambda b, i: (0, 0)),
        ],
        out_specs=pl.BlockSpec((None, rows, D_MODEL), lambda b, i: (b, i, 0)),
        out_shape=jax.ShapeDtypeStruct(x.shape, F32),
        scratch_shapes=[
            pltpu.VMEM((2, seq, rows), F32),
            pltpu.VMEM((2, 1, rows), F32),
            pltpu.VMEM((2, VT_ROWS, rows), F32),
            pltpu.VMEM((N_HEADS * V_HEAD, rows), BF16),
        ],
        compiler_params=_params(2),
        name="mla_attn",
    )(qt, k, vt, x, mod, g_pair, w_uvt, w_ot)


def kernel(x, c, ada_w, ada_b, norm_g, ffn_w_in, ffn_w_out, pool_w, pool_b, pool_scale,
           mla_w_in, mla_q_norm, mla_kv_norm, mla_w_uq, mla_w_uk, mla_w_uv, mla_w_o):
    batch, seq, _ = x.shape
    mods = _ada_mod(c, ada_w, ada_b).reshape(DEPTH, batch, 3, 3, D_MODEL)
    tables = _rope_tables(seq)
    w_in_all = _ffn_w_in_prep(ffn_w_in)
    w_out_all = ffn_w_out.astype(BF16)
    for i in range(DEPTH):
        mod = mods[i]
        g = norm_g[i]
        li = i // 2
        x = _ffn_sublayer(x, mod, 0, g[0:2], w_in_all, w_out_all, i, 0)
        if i % 2 == 0:
            x = _pool_sublayer(x, mod, 1, g[2:4], pool_w[li], pool_b[li], pool_scale[li])
        else:
            w_lat, wqt, w_uvt, w_ot = _mla_weights(
                mla_w_in[li], mla_w_uq[li], mla_w_uk[li], mla_w_uv[li], mla_w_o[li])
            qt, k, vt = _mla_pre(x, mod, 1, g[2:4], w_lat, mla_q_norm[li], mla_kv_norm[li],
                                 wqt, tables)
            x = _mla_attn(x, qt, k, vt, mod, 1, g[2:4], w_uvt, w_ot)
        x = _ffn_sublayer(x, mod, 2, g[4:6], w_in_all, w_out_all, i, 1)
    return x
```

```python
import jax
import jax.numpy as jnp
import numpy as np
from jax import lax
from jax.experimental import pallas as pl
from jax.experimental.pallas import tpu as pltpu

F32 = jnp.float32
BF16 = jnp.bfloat16

D_MODEL = 1024
DEPTH = 2
N_POOL_GROUPS = 4
POOL_GROUP = D_MODEL // N_POOL_GROUPS
POOL_WINDOWS = (2, 4, 8, 16)
N_HEADS = 16
QK_NOPE = 64
QK_ROPE = 32
V_HEAD = 64
Q_LORA = D_MODEL // 4
KV_LORA = D_MODEL // 8
ROPE_THETA = 10000.0
D_FF = 11 * D_MODEL // 4
EPS = 1e-6
N_MOD = 9
ATTN_SCALE = (QK_NOPE + QK_ROPE) ** -0.5
LOG2_E = 1.4426950408889634
Q_SCALE = ATTN_SCALE * LOG2_E

LANES = 128
SUBLANES = 8
MXU_DIM = 256
VMEM_LIMIT_BYTES = 56 * 1024 * 1024

FFN_CHUNK = MXU_DIM
N_FFN_CHUNKS = D_FF // FFN_CHUNK
FFN_ROWS = 1024
FFN_SUB_ROWS = 256
FFN_SUB_SPLITS = (FFN_SUB_ROWS,) * (FFN_ROWS // FFN_SUB_ROWS)
POOL_ROWS = 1024
POOL_HALO = 2 * SUBLANES
POOL_BLOCK = 128
POOL_BAND_K = POOL_BLOCK + 2 * POOL_HALO
MLA_PRE_ROWS = 1024
ATTN_ROWS = 512
KEY_TILE = MXU_DIM
SCORE_LEAD = 2
EMIT_AT = SCORE_LEAD - 1
HEADS_PER_TRIP = 4
QK_WIDTH = 2 * LANES
HALF_ROPE = QK_ROPE // 2
QK_FEATS = KV_LORA + QK_ROPE
BF16_SUBLANES = 2 * SUBLANES
VT_ROWS = KV_LORA + BF16_SUBLANES


def _params(n_axes, flags=None):
    return pltpu.CompilerParams(
        dimension_semantics=("arbitrary",) * n_axes,
        vmem_limit_bytes=VMEM_LIMIT_BYTES,
        flags=flags,
    )


def _resident(block_shape, index_map):
    return pl.BlockSpec(block_shape, index_map, pipeline_mode=pl.Buffered(1))


def _rms(v):
    return v * lax.rsqrt(jnp.mean(v * v, axis=-1, keepdims=True) + EPS)


def _modulated_norm(x, mod, g_pre):
    return _rms(x) * (g_pre * (1.0 + mod[1:2])) + mod[0:1]


def _residual(x, y, mod, g_post, weight):
    return x + (weight * (1.0 + mod[2:3])) * (_rms(y) * g_post)


ADA_COLS = 1152


def _ada_kernel(c_ref, w_ref, b_ref, o_ref):
    c = c_ref[...]
    sc = c * (1.0 / (1.0 + jnp.exp(-c)))
    o_ref[...] = jnp.dot(sc.astype(BF16), w_ref[...].astype(BF16),
                         preferred_element_type=F32) + b_ref[...]


def _ada_mod(c, ada_w, ada_b):
    batch = c.shape[0]
    n_out = N_MOD * D_MODEL
    return pl.pallas_call(
        _ada_kernel,
        grid=(DEPTH, n_out // ADA_COLS),
        in_specs=[
            pl.BlockSpec((batch, D_MODEL), lambda l, j: (0, 0)),
            pl.BlockSpec((None, D_MODEL, ADA_COLS), lambda l, j: (l, 0, j)),
            pl.BlockSpec((None, 1, ADA_COLS), lambda l, j: (l, 0, j)),
        ],
        out_specs=pl.BlockSpec((None, batch, ADA_COLS), lambda l, j: (l, 0, j)),
        out_shape=jax.ShapeDtypeStruct((DEPTH, batch, n_out), F32),
        compiler_params=_params(2),
        name="ada_mod",
    )(c, ada_w, ada_b.reshape(DEPTH, 1, n_out))


def _ffn_kernel(x_ref, mod_ref, g_ref, win_ref, wout_ref, o_ref, act_ref):
    mod = mod_ref[...]
    g = g_ref[...]
    n_sub = len(FFN_SUB_SPLITS)
    starts = [sum(FFN_SUB_SPLITS[:s]) for s in range(n_sub)]

    def rows_of(s):
        return pl.ds(starts[s], FFN_SUB_SPLITS[s])

    def normed(s, after=None):
        x = x_ref[rows_of(s), :]
        ms = jnp.mean(x * x, axis=-1, keepdims=True) + EPS
        if after is not None:
            bits = pltpu.bitcast(after, jnp.uint32)
            ms = ms + pltpu.bitcast((bits >> 16) >> 16, F32)
        return ((x * lax.rsqrt(ms)) * (g[0:1] * (1.0 + mod[1:2])) + mod[0:1]).astype(BF16)

    def finish(s):
        o_ref[rows_of(s), :] = _residual(x_ref[rows_of(s), :], o_ref[rows_of(s), :], mod,
                                         g[1:2], 0.5)

    hb = normed(0)
    for s in range(n_sub):
        hb_next = None
        for c in range(N_FFN_CHUNKS):
            gu = jnp.dot(hb, win_ref[:, 2 * c * FFN_CHUNK:2 * (c + 1) * FFN_CHUNK],
                         preferred_element_type=F32)
            gate = gu[:, :FFN_CHUNK]
            up = gu[:, FFN_CHUNK:]
            act = gate * (1.0 / (1.0 + jnp.exp(-gate))) * up
            act_ref[s % 2, 0:FFN_SUB_SPLITS[s], c * FFN_CHUNK:(c + 1) * FFN_CHUNK] = act.astype(BF16)
            if c == 3 and s + 1 < n_sub:
                hb_next = normed(s + 1, after=gu[0:1, 0:1])
            if c == 7 and s >= 1:
                finish(s - 1)
        o_ref[rows_of(s), :] = jnp.dot(act_ref[s % 2, 0:FFN_SUB_SPLITS[s], :], wout_ref[...],
                                       preferred_element_type=F32)
        hb = hb_next
    finish(n_sub - 1)


def _ffn_sublayer(x, mod, sub, g_pair, w_in_all, w_out_all, layer, which):
    batch, seq, _ = x.shape
    return pl.pallas_call(
        _ffn_kernel,
        grid=(batch, seq // FFN_ROWS),
        in_specs=[
            pl.BlockSpec((None, FFN_ROWS, D_MODEL), lambda b, i: (b, i, 0)),
            pl.BlockSpec((None, None, 3, D_MODEL), lambda b, i: (b, sub, 0, 0)),
            pl.BlockSpec((2, D_MODEL), lambda b, i: (0, 0)),
            _resident((None, None, D_MODEL, 2 * D_FF), lambda b, i: (layer, which, 0, 0)),
            _resident((None, None, D_FF, D_MODEL), lambda b, i: (layer, which, 0, 0)),
        ],
        out_specs=pl.BlockSpec((None, FFN_ROWS, D_MODEL), lambda b, i: (b, i, 0)),
        out_shape=jax.ShapeDtypeStruct(x.shape, F32),
        scratch_shapes=[pltpu.VMEM((2, FFN_SUB_ROWS, D_FF), BF16)],
        compiler_params=_params(2),
        name="ffn_sublayer",
    )(x, mod, g_pair, w_in_all, w_out_all)


def _w_in_prep_kernel(gate_ref, up_ref, o_ref):
    o_ref[:, :FFN_CHUNK] = gate_ref[...].astype(BF16)
    o_ref[:, FFN_CHUNK:] = up_ref[...].astype(BF16)


def _ffn_w_in_prep(ffn_w_in):
    chunk_spec = lambda offset: pl.BlockSpec(
        (None, None, D_MODEL, FFN_CHUNK), lambda l, j, c: (l, j, 0, offset + c))
    return pl.pallas_call(
        _w_in_prep_kernel,
        grid=(DEPTH, 2, N_FFN_CHUNKS),
        in_specs=[chunk_spec(0), chunk_spec(N_FFN_CHUNKS)],
        out_specs=pl.BlockSpec((None, None, D_MODEL, 2 * FFN_CHUNK), lambda l, j, c: (l, j, 0, c)),
        out_shape=jax.ShapeDtypeStruct(ffn_w_in.shape, BF16),
        compiler_params=_params(3),
        name="ffn_w_in_prep",
    )(ffn_w_in, ffn_w_in)


def _pool_bands():
    r = np.arange(POOL_BLOCK)[:, None]
    c = np.arange(POOL_BAND_K)[None, :]
    bands = [(c >= r + POOL_HALO - w // 2) & (c < r + POOL_HALO + w // 2) for w in POOL_WINDOWS]
    return jnp.asarray(np.stack(bands).astype(np.float32), dtype=BF16)


def _pool_edge_ratios(seq):
    ratios = np.ones((2, POOL_HALO, D_MODEL))
    for gi, w in enumerate(POOL_WINDOWS):
        lanes = slice(gi * POOL_GROUP, (gi + 1) * POOL_GROUP)
        for j in range(POOL_HALO):
            for side, t in enumerate((j, seq - POOL_HALO + j)):
                count = min(t + w // 2, seq) - max(t - w // 2, 0)
                ratios[side, j, lanes] = w / count
    return jnp.asarray(ratios.astype(np.float32))


def _pool_kernel(x_ref, xp_ref, xn_ref, mod_ref, g_ref, band_ref, edge_ref, w_ref, b_ref, s_ref,
                 o_ref, hi_ref, lo_ref, mean_ref):
    i = pl.program_id(1)
    n_i = pl.num_programs(1)
    x = x_ref[...]
    mod = mod_ref[...]
    g = g_ref[...]
    rows = x.shape[0]
    h = _modulated_norm(x, mod, g[0:1])

    def store_split(first_row, v):
        hi = v.astype(BF16)
        hi_ref[first_row:first_row + v.shape[0], :] = hi
        lo_ref[first_row:first_row + v.shape[0], :] = (v - hi.astype(F32)).astype(BF16)

    store_split(0, _modulated_norm(xp_ref[...], mod, g[0:1]) * (i > 0).astype(F32))
    store_split(POOL_HALO, h)
    store_split(POOL_HALO + rows, _modulated_norm(xn_ref[...], mod, g[0:1])
                * (i < n_i - 1).astype(F32))

    for gi, window in enumerate(POOL_WINDOWS):
        lanes = slice(gi * POOL_GROUP, (gi + 1) * POOL_GROUP)
        band = band_ref[gi]
        for blk in range(rows // POOL_BLOCK):
            src = pl.ds(blk * POOL_BLOCK, POOL_BAND_K)
            tot = (jnp.dot(band, hi_ref[src, lanes], preferred_element_type=F32)
                   + jnp.dot(band, lo_ref[src, lanes], preferred_element_type=F32))
            mean_ref[blk * POOL_BLOCK:(blk + 1) * POOL_BLOCK, lanes] = tot * (1.0 / window)

    @pl.when(i == 0)
    def _():
        mean_ref[0:POOL_HALO, :] = mean_ref[0:POOL_HALO, :] * edge_ref[0]

    @pl.when(i == n_i - 1)
    def _():
        mean_ref[rows - POOL_HALO:rows, :] = mean_ref[rows - POOL_HALO:rows, :] * edge_ref[1]

    diff = (mean_ref[...] - h).astype(BF16)
    ys = [jnp.dot(diff[:, gi * POOL_GROUP:(gi + 1) * POOL_GROUP], w_ref[gi],
                  preferred_element_type=F32) for gi in range(N_POOL_GROUPS)]
    y = (jnp.concatenate(ys, axis=-1) + b_ref[...]) * s_ref[...]
    o_ref[...] = _residual(x, y, mod, g[1:2], 1.0)


def _pool_sublayer(x, mod, sub, g_pair, pool_w, pool_b, pool_scale):
    batch, seq, _ = x.shape
    halo_blocks = POOL_ROWS // POOL_HALO
    n_halo = seq // POOL_HALO
    return pl.pallas_call(
        _pool_kernel,
        grid=(batch, seq // POOL_ROWS),
        in_specs=[
            pl.BlockSpec((None, POOL_ROWS, D_MODEL), lambda b, i: (b, i, 0)),
            pl.BlockSpec((None, POOL_HALO, D_MODEL),
                         lambda b, i: (b, jnp.maximum(i * halo_blocks - 1, 0), 0)),
            pl.BlockSpec((None, POOL_HALO, D_MODEL),
                         lambda b, i: (b, jnp.minimum((i + 1) * halo_blocks, n_halo - 1), 0)),
            pl.BlockSpec((None, None, 3, D_MODEL), lambda b, i: (b, sub, 0, 0)),
            pl.BlockSpec((2, D_MODEL), lambda b, i: (0, 0)),
            pl.BlockSpec((N_POOL_GROUPS, POOL_BLOCK, POOL_BAND_K), lambda b, i: (0, 0, 0)),
            pl.BlockSpec((2, POOL_HALO, D_MODEL), lambda b, i: (0, 0, 0)),
            pl.BlockSpec((N_POOL_GROUPS, POOL_GROUP, POOL_GROUP), lambda b, i: (0, 0, 0)),
            pl.BlockSpec((1, D_MODEL), lambda b, i: (0, 0)),
            pl.BlockSpec((1, D_MODEL), lambda b, i: (0, 0)),
        ],
        out_specs=pl.BlockSpec((None, POOL_ROWS, D_MODEL), lambda b, i: (b, i, 0)),
        out_shape=jax.ShapeDtypeStruct(x.shape, F32),
        scratch_shapes=[
            pltpu.VMEM((POOL_ROWS + 2 * POOL_HALO, D_MODEL), BF16),
            pltpu.VMEM((POOL_ROWS + 2 * POOL_HALO, D_MODEL), BF16),
            pltpu.VMEM((POOL_ROWS, D_MODEL), F32),
        ],
        compiler_params=_params(2),
        name="pool_sublayer",
    )(x, x, x, mod, g_pair, _pool_bands(), _pool_edge_ratios(seq), pool_w.astype(BF16),
      pool_b.reshape(1, D_MODEL), pool_scale.reshape(1, D_MODEL))


LAT_COLS = Q_LORA + KV_LORA + 2 * LANES


def _fold_qk_kernel(wuq_ref, wuk_ref, o_ref):
    o_ref[...] = Q_SCALE * lax.dot_general(
        wuq_ref[:, :QK_NOPE], wuk_ref[...], (((1,), (1,)), ((), ())),
        preferred_element_type=F32, precision=lax.Precision.HIGHEST)


def _mla_weights(w_in, w_uq, w_uk, w_uv, w_o):
    wuq_h = w_uq.transpose(1, 0, 2)
    wuk_h = w_uk.transpose(1, 0, 2)
    w_ql = pl.pallas_call(
        _fold_qk_kernel,
        grid=(N_HEADS,),
        in_specs=[
            pl.BlockSpec((None, Q_LORA, QK_NOPE + QK_ROPE), lambda h: (h, 0, 0)),
            pl.BlockSpec((None, KV_LORA, QK_NOPE), lambda h: (h, 0, 0)),
        ],
        out_specs=pl.BlockSpec((None, Q_LORA, KV_LORA), lambda h: (h, 0, 0)),
        out_shape=jax.ShapeDtypeStruct((N_HEADS, Q_LORA, KV_LORA), F32),
        compiler_params=_params(1),
        name="fold_qk",
    )(wuq_h, wuk_h)

    def rope_cols(w, swap):
        a, b = w[..., :HALF_ROPE], w[..., HALF_ROPE:]
        first, second = (b, a) if swap else (a, b)
        pad = jnp.zeros(w.shape[:-1] + (LANES - QK_ROPE,), w.dtype)
        return jnp.concatenate([first, second, pad], axis=-1)

    wqt = jnp.concatenate([w_ql, wuq_h[..., QK_NOPE:]], axis=-1).transpose(0, 2, 1).astype(BF16)
    wk_rope = w_in[:, Q_LORA + KV_LORA:]
    w_lat = jnp.concatenate(
        [w_in[:, :Q_LORA + KV_LORA], rope_cols(wk_rope, False), rope_cols(wk_rope, True)],
        axis=-1).astype(BF16)
    w_uvt = w_uv.transpose(1, 2, 0).astype(BF16)
    return w_lat, wqt, w_uvt, w_o.astype(BF16)


def _rope_tables(seq):
    inv = 1.0 / (ROPE_THETA ** (np.arange(0, QK_ROPE, 2, dtype=np.float64) / QK_ROPE))
    ang = np.arange(seq, dtype=np.float64)[:, None] * inv[None, :]
    cos, sin = np.cos(ang), np.sin(ang)
    pad = np.zeros((seq, LANES - QK_ROPE))
    as_f32 = lambda a: jnp.asarray(a.astype(np.float32))
    return (as_f32(np.concatenate([cos, cos, pad], axis=-1)),
            as_f32(np.concatenate([-sin, sin, pad], axis=-1)),
            as_f32(cos.T * Q_SCALE), as_f32(sin.T * Q_SCALE))


def _mla_pre_kernel(x_ref, mod_ref, g_ref, wlat_ref, qn_ref, kvn_ref, wqt_ref,
                    cosk_ref, sink_ref, cosq_ref, sinq_ref, qt_ref, k_ref, vt_ref):
    x = x_ref[...]
    rows = x.shape[0]
    hb = _modulated_norm(x, mod_ref[...], g_ref[0:1, :]).astype(BF16)
    lat = jnp.dot(hb, wlat_ref[...], preferred_element_type=F32)
    c_q = _rms(lat[:, :Q_LORA]) * qn_ref[...]
    c_kv = _rms(lat[:, Q_LORA:Q_LORA + KV_LORA]) * kvn_ref[...]
    k_rope = (lat[:, Q_LORA + KV_LORA:Q_LORA + KV_LORA + LANES] * cosk_ref[...]
              + lat[:, Q_LORA + KV_LORA + LANES:] * sink_ref[...])
    k_ref[...] = jnp.concatenate([c_kv, k_rope], axis=-1).astype(BF16)
    vt_ref[0:KV_LORA, :] = c_kv.T.astype(BF16)
    ones_row = lax.broadcasted_iota(jnp.int32, (VT_ROWS - KV_LORA, rows), 0) == 0
    vt_ref[KV_LORA:, :] = ones_row.astype(BF16)
    cqt = c_q.T.astype(BF16)
    cos_q = cosq_ref[...]
    sin_q = sinq_ref[...]
    r1 = KV_LORA + HALF_ROPE
    for head in range(N_HEADS):
        a = jnp.dot(wqt_ref[head], cqt, preferred_element_type=F32)
        x1 = a[KV_LORA:r1]
        x2 = a[r1:]
        qt_ref[head, 0:KV_LORA, :] = a[:KV_LORA].astype(BF16)
        qt_ref[head, KV_LORA:r1, :] = (x1 * cos_q - x2 * sin_q).astype(BF16)
        qt_ref[head, r1:QK_FEATS, :] = (x2 * cos_q + x1 * sin_q).astype(BF16)


def _mla_pre(x, mod, sub, g_pair, w_lat, q_norm, kv_norm, wqt, tables):
    batch, seq, _ = x.shape
    rows = MLA_PRE_ROWS
    return pl.pallas_call(
        _mla_pre_kernel,
        grid=(batch, seq // rows),
        in_specs=[
            pl.BlockSpec((None, rows, D_MODEL), lambda b, i: (b, i, 0)),
            pl.BlockSpec((None, None, 3, D_MODEL), lambda b, i: (b, sub, 0, 0)),
            pl.BlockSpec((2, D_MODEL), lambda b, i: (0, 0)),
            pl.BlockSpec((D_MODEL, LAT_COLS), lambda b, i: (0, 0)),
            pl.BlockSpec((1, Q_LORA), lambda b, i: (0, 0)),
            pl.BlockSpec((1, KV_LORA), lambda b, i: (0, 0)),
            pl.BlockSpec((N_HEADS, QK_FEATS, Q_LORA), lambda b, i: (0, 0, 0)),
            pl.BlockSpec((rows, LANES), lambda b, i: (i, 0)),
            pl.BlockSpec((rows, LANES), lambda b, i: (i, 0)),
            pl.BlockSpec((HALF_ROPE, rows), lambda b, i: (0, i)),
            pl.BlockSpec((HALF_ROPE, rows), lambda b, i: (0, i)),
        ],
        out_specs=[
            pl.BlockSpec((None, N_HEADS, QK_FEATS, rows), lambda b, i: (b, 0, 0, i)),
            pl.BlockSpec((None, rows, QK_WIDTH), lambda b, i: (b, i, 0)),
            pl.BlockSpec((None, VT_ROWS, rows), lambda b, i: (b, 0, i)),
        ],
        out_shape=[
            jax.ShapeDtypeStruct((batch, N_HEADS, QK_FEATS, seq), BF16),
            jax.ShapeDtypeStruct((batch, seq, QK_WIDTH), BF16),
            jax.ShapeDtypeStruct((batch, VT_ROWS, seq), BF16),
        ],
        compiler_params=_params(2),
        name="mla_pre",
    )(x, mod, g_pair, w_lat, q_norm.reshape(1, Q_LORA), kv_norm.reshape(1, KV_LORA),
      wqt, *tables)


def _mla_attn_kernel(qt_ref, k_ref, vt_ref, x_ref, mod_ref, g_ref, wuvt_ref, wo_ref, o_ref,
                     st_ref, m_ref, acc_ref, ov_ref):
    seq = k_ref.shape[0]
    n_tiles = seq // KEY_TILE

    def score_tile(qt_next, slot, kt, mx):
        keys = pl.ds(kt * KEY_TILE, KEY_TILE)
        st = jnp.dot(k_ref[keys, 0:QK_FEATS], qt_next, preferred_element_type=F32)
        st_ref[slot, keys, :] = st
        tmax = jnp.max(st.reshape(KEY_TILE // SUBLANES, SUBLANES, st.shape[1]), axis=0)
        return tmax if mx is None else jnp.maximum(mx, tmax)

    def prob_tile(slot, kt, m, acc):
        keys = pl.ds(kt * KEY_TILE, KEY_TILE)
        pt = jnp.exp2(st_ref[slot, keys, :] - m).astype(BF16)
        part = jnp.dot(vt_ref[:, keys], pt, preferred_element_type=F32)
        return part if acc is None else acc + part

    def scores(head, slot):
        qt_next = qt_ref[head]
        mx = None
        for kt in range(n_tiles):
            mx = score_tile(qt_next, slot, kt, mx)
        m_ref[slot] = jnp.max(mx, axis=0, keepdims=True)

    def emit(head, slot):
        acc = acc_ref[slot]
        ot = acc[:KV_LORA] * (1.0 / acc[KV_LORA:KV_LORA + 1])
        ovt = jnp.dot(wuvt_ref[head], ot.astype(BF16), preferred_element_type=F32)
        first = head * V_HEAD
        if not isinstance(head, int):
            first = pl.multiple_of(first, V_HEAD)
        ov_ref[pl.ds(first, V_HEAD), :] = ovt.astype(BF16)

    def overlapped(head, slot, emit_previous=True, score_next=True):
        qt_next = qt_ref[head + 1] if score_next else None
        m = m_ref[slot]
        acc = None
        mx = None
        for kt in range(n_tiles + SCORE_LEAD):
            if score_next and kt < n_tiles:
                mx = score_tile(qt_next, 1 - slot, kt, mx)
            if emit_previous and kt == EMIT_AT:
                emit(head - 1, 1 - slot)
            if kt >= SCORE_LEAD:
                acc = prob_tile(slot, kt - SCORE_LEAD, m, acc)
        if score_next:
            m_ref[1 - slot] = jnp.max(mx, axis=0, keepdims=True)
        acc_ref[slot] = acc

    def quad_body(j, carry):
        for r in range(HEADS_PER_TRIP):
            overlapped(HEADS_PER_TRIP * j + 1 + r, (1 + r) % 2)
        return carry

    scores(0, 0)
    overlapped(0, 0, emit_previous=False)
    n_trips = (N_HEADS - 2) // HEADS_PER_TRIP
    lax.fori_loop(0, n_trips, quad_body, 0)
    for head in range(1 + n_trips * HEADS_PER_TRIP, N_HEADS - 1):
        overlapped(head, head % 2)
    overlapped(N_HEADS - 1, 1, score_next=False)
    emit(N_HEADS - 1, 1)
    y = lax.dot_general(ov_ref[...], wo_ref[...], (((0,), (0,)), ((), ())),
                        preferred_element_type=F32)
    o_ref[...] = _residual(x_ref[...], y, mod_ref[...], g_ref[1:2, :], 1.0)


def _mla_attn(x, qt, k, vt, mod, sub, g_pair, w_uvt, w_ot):
    batch, seq, _ = x.shape
    rows = ATTN_ROWS
    n_q = seq // rows
    return pl.pallas_call(
        _mla_attn_kernel,
        grid=(batch, n_q),
        in_specs=[
            pl.BlockSpec((None, N_HEADS, QK_FEATS, rows), lambda b, i: (b, 0, 0, i)),
            pl.BlockSpec((None, seq, QK_WIDTH), lambda b, i: (b, 0, 0)),
            pl.BlockSpec((None, VT_ROWS, seq), lambda b, i: (b, 0, 0)),
            pl.BlockSpec((None, rows, D_MODEL), lambda b, i: (b, i, 0)),
            pl.BlockSpec((None, None, 3, D_MODEL), lambda b, i: (b, sub, 0, 0)),
            pl.BlockSpec((2, D_MODEL), lambda b, i: (0, 0)),
            _resident((N_HEADS, V_HEAD, KV_LORA), lambda b, i: (0, 0, 0)),
            _resident((D_MODEL, N_HEADS * V_HEAD), lambda b, i: (0, 0)),
        ],
        out_specs=pl.BlockSpec((None, rows, D_MODEL), lambda b, i: (b, i, 0)),
        out_shape=jax.ShapeDtypeStruct(x.shape, F32),
        scratch_shapes=[
            pltpu.VMEM((2, seq, rows), F32),
            pltpu.VMEM((2, 1, rows), F32),
            pltpu.VMEM((2, VT_ROWS, rows), F32),
            pltpu.VMEM((N_HEADS * V_HEAD, rows), BF16),
        ],
        compiler_params=_params(2),
        name="mla_attn",
    )(qt, k, vt, x, mod, g_pair, w_uvt, w_ot)


def kernel(x, c, ada_w, ada_b, norm_g, ffn_w_in, ffn_w_out, pool_w, pool_b, pool_scale,
           mla_w_in, mla_q_norm, mla_kv_norm, mla_w_uq, mla_w_uk, mla_w_uv, mla_w_o):
    batch, seq, _ = x.shape
    mods = _ada_mod(c, ada_w, ada_b).reshape(DEPTH, batch, 3, 3, D_MODEL)
    tables = _rope_tables(seq)
    w_in_all = _ffn_w_in_prep(ffn_w_in)
    w_out_all = ffn_w_out.astype(BF16)
    for i in range(DEPTH):
        mod = mods[i]
        g = norm_g[i]
        li = i // 2
        x = _ffn_sublayer(x, mod, 0, g[0:2], w_in_all, w_out_all, i, 0)
        if i % 2 == 0:
            x = _pool_sublayer(x, mod, 1, g[2:4], pool_w[li], pool_b[li], pool_scale[li])
        else:
            w_lat, wqt, w_uvt, w_ot = _mla_weights(
                mla_w_in[li], mla_w_uq[li], mla_w_uk[li], mla_w_uv[li], mla_w_o[li])
            qt, k, vt = _mla_pre(x, mod, 1, g[2:4], w_lat, mla_q_norm[li], mla_kv_norm[li],
                                 wqt, tables)
            x = _mla_attn(x, qt, k, vt, mod, 1, g[2:4], w_uvt, w_ot)
        x = _ffn_sublayer(x, mod, 2, g[4:6], w_in_all, w_out_all, i, 1)
    return x
```

```python
import jax
import jax.numpy as jnp
import numpy as np
from jax import lax
from jax.experimental import pallas as pl
from jax.experimental.pallas import tpu as pltpu

F32 = jnp.float32
BF16 = jnp.bfloat16

D_MODEL = 1024
DEPTH = 2
N_POOL_GROUPS = 4
POOL_GROUP = D_MODEL // N_POOL_GROUPS
POOL_WINDOWS = (2, 4, 8, 16)
N_HEADS = 16
QK_NOPE = 64
QK_ROPE = 32
V_HEAD = 64
Q_LORA = D_MODEL // 4
KV_LORA = D_MODEL // 8
ROPE_THETA = 10000.0
D_FF = 11 * D_MODEL // 4
EPS = 1e-6
N_MOD = 9
ATTN_SCALE = (QK_NOPE + QK_ROPE) ** -0.5
LOG2_E = 1.4426950408889634
Q_SCALE = ATTN_SCALE * LOG2_E

LANES = 128
SUBLANES = 8
MXU_DIM = 256
VMEM_LIMIT_BYTES = 56 * 1024 * 1024

FFN_CHUNK = MXU_DIM
N_FFN_CHUNKS = D_FF // FFN_CHUNK
FFN_ROWS = 1024
FFN_SUB_ROWS = 256
FFN_SUB_SPLITS = (FFN_SUB_ROWS,) * (FFN_ROWS // FFN_SUB_ROWS)
POOL_ROWS = 1024
POOL_HALO = 2 * SUBLANES
POOL_BLOCK = 128
POOL_BAND_K = POOL_BLOCK + 2 * POOL_HALO
MLA_PRE_ROWS = 1024
ATTN_ROWS = 512
KEY_TILE = MXU_DIM
SCORE_LEAD = 2
EMIT_AT = SCORE_LEAD - 1
HEADS_PER_TRIP = 4
QK_WIDTH = 2 * LANES
HALF_ROPE = QK_ROPE // 2
QK_FEATS = KV_LORA + QK_ROPE
BF16_SUBLANES = 2 * SUBLANES
VT_ROWS = KV_LORA + BF16_SUBLANES


def _params(n_axes, flags=None):
    return pltpu.CompilerParams(
        dimension_semantics=("arbitrary",) * n_axes,
        vmem_limit_bytes=VMEM_LIMIT_BYTES,
        flags=flags,
    )


def _resident(block_shape, index_map):
    return pl.BlockSpec(block_shape, index_map, pipeline_mode=pl.Buffered(1))


def _rms(v):
    return v * lax.rsqrt(jnp.mean(v * v, axis=-1, keepdims=True) + EPS)


def _modulated_norm(x, mod, g_pre):
    return _rms(x) * (g_pre * (1.0 + mod[1:2])) + mod[0:1]


def _residual(x, y, mod, g_post, weight):
    return x + (weight * (1.0 + mod[2:3])) * (_rms(y) * g_post)


ADA_COLS = 1152


def _ada_kernel(c_ref, w_ref, b_ref, o_ref):
    c = c_ref[...]
    sc = c * (1.0 / (1.0 + jnp.exp(-c)))
    o_ref[...] = jnp.dot(sc.astype(BF16), w_ref[...].astype(BF16),
                         preferred_element_type=F32) + b_ref[...]


def _ada_mod(c, ada_w, ada_b):
    batch = c.shape[0]
    n_out = N_MOD * D_MODEL
    return pl.pallas_call(
        _ada_kernel,
        grid=(DEPTH, n_out // ADA_COLS),
        in_specs=[
            pl.BlockSpec((batch, D_MODEL), lambda l, j: (0, 0)),
            pl.BlockSpec((None, D_MODEL, ADA_COLS), lambda l, j: (l, 0, j)),
            pl.BlockSpec((None, 1, ADA_COLS), lambda l, j: (l, 0, j)),
        ],
        out_specs=pl.BlockSpec((None, batch, ADA_COLS), lambda l, j: (l, 0, j)),
        out_shape=jax.ShapeDtypeStruct((DEPTH, batch, n_out), F32),
        compiler_params=_params(2),
        name="ada_mod",
    )(c, ada_w, ada_b.reshape(DEPTH, 1, n_out))


def _ffn_kernel(x_ref, mod_ref, g_ref, win_ref, wout_ref, o_ref, act_ref):
    mod = mod_ref[...]
    g = g_ref[...]
    n_sub = len(FFN_SUB_SPLITS)
    starts = [sum(FFN_SUB_SPLITS[:s]) for s in range(n_sub)]

    def rows_of(s):
        return pl.ds(starts[s], FFN_SUB_SPLITS[s])

    def normed(s, after=None):
        x = x_ref[rows_of(s), :]
        ms = jnp.mean(x * x, axis=-1, keepdims=True) + EPS
        if after is not None:
            bits = pltpu.bitcast(after, jnp.uint32)
            ms = ms + pltpu.bitcast((bits >> 16) >> 16, F32)
        return ((x * lax.rsqrt(ms)) * (g[0:1] * (1.0 + mod[1:2])) + mod[0:1]).astype(BF16)

    def finish(s):
        o_ref[rows_of(s), :] = _residual(x_ref[rows_of(s), :], o_ref[rows_of(s), :], mod,
                                         g[1:2], 0.5)

    hb = normed(0)
    for s in range(n_sub):
        hb_next = None
        for c in range(N_FFN_CHUNKS):
            gu = jnp.dot(hb, win_ref[:, 2 * c * FFN_CHUNK:2 * (c + 1) * FFN_CHUNK],
                         preferred_element_type=F32)
            gate = gu[:, :FFN_CHUNK]
            up = gu[:, FFN_CHUNK:]
            act = gate * (1.0 / (1.0 + jnp.exp(-gate))) * up
            act_ref[s % 2, 0:FFN_SUB_SPLITS[s], c * FFN_CHUNK:(c + 1) * FFN_CHUNK] = act.astype(BF16)
            if c == 3 and s + 1 < n_sub:
                hb_next = normed(s + 1, after=gu[0:1, 0:1])
            if c == 7 and s >= 1:
                finish(s - 1)
        o_ref[rows_of(s), :] = jnp.dot(act_ref[s % 2, 0:FFN_SUB_SPLITS[s], :], wout_ref[...],
                                       preferred_element_type=F32)
        hb = hb_next
    finish(n_sub - 1)


def _ffn_sublayer(x, mod, sub, g_pair, w_in_all, w_out_all, layer, which):
    batch, seq, _ = x.shape
    return pl.pallas_call(
        _ffn_kernel,
        grid=(batch, seq // FFN_ROWS),
        in_specs=[
            pl.BlockSpec((None, FFN_ROWS, D_MODEL), lambda b, i: (b, i, 0)),
            pl.BlockSpec((None, None, 3, D_MODEL), lambda b, i: (b, sub, 0, 0)),
            pl.BlockSpec((2, D_MODEL), lambda b, i: (0, 0)),
            _resident((None, None, D_MODEL, 2 * D_FF), lambda b, i: (layer, which, 0, 0)),
            _resident((None, None, D_FF, D_MODEL), lambda b, i: (layer, which, 0, 0)),
        ],
        out_specs=pl.BlockSpec((None, FFN_ROWS, D_MODEL), lambda b, i: (b, i, 0)),
        out_shape=jax.ShapeDtypeStruct(x.shape, F32),
        scratch_shapes=[pltpu.VMEM((2, FFN_SUB_ROWS, D_FF), BF16)],
        compiler_params=_params(2),
        name="ffn_sublayer",
    )(x, mod, g_pair, w_in_all, w_out_all)


def _w_in_prep_kernel(gate_ref, up_ref, o_ref):
    o_ref[:, :FFN_CHUNK] = gate_ref[...].astype(BF16)
    o_ref[:, FFN_CHUNK:] = up_ref[...].astype(BF16)


def _ffn_w_in_prep(ffn_w_in):
    chunk_spec = lambda offset: pl.BlockSpec(
        (None, None, D_MODEL, FFN_CHUNK), lambda l, j, c: (l, j, 0, offset + c))
    return pl.pallas_call(
        _w_in_prep_kernel,
        grid=(DEPTH, 2, N_FFN_CHUNKS),
        in_specs=[chunk_spec(0), chunk_spec(N_FFN_CHUNKS)],
        out_specs=pl.BlockSpec((None, None, D_MODEL, 2 * FFN_CHUNK), lambda l, j, c: (l, j, 0, c)),
        out_shape=jax.ShapeDtypeStruct(ffn_w_in.shape, BF16),
        compiler_params=_params(3),
        name="ffn_w_in_prep",
    )(ffn_w_in, ffn_w_in)


def _pool_bands():
    r = np.arange(POOL_BLOCK)[:, None]
    c = np.arange(POOL_BAND_K)[None, :]
    bands = [(c >= r + POOL_HALO - w // 2) & (c < r + POOL_HALO + w // 2) for w in POOL_WINDOWS]
    return jnp.asarray(np.stack(bands).astype(np.float32), dtype=BF16)


def _pool_edge_ratios(seq):
    ratios = np.ones((2, POOL_HALO, D_MODEL))
    for gi, w in enumerate(POOL_WINDOWS):
        lanes = slice(gi * POOL_GROUP, (gi + 1) * POOL_GROUP)
        for j in range(POOL_HALO):
            for side, t in enumerate((j, seq - POOL_HALO + j)):
                count = min(t + w // 2, seq) - max(t - w // 2, 0)
                ratios[side, j, lanes] = w / count
    return jnp.asarray(ratios.astype(np.float32))


def _pool_kernel(x_ref, xp_ref, xn_ref, mod_ref, g_ref, band_ref, edge_ref, w_ref, b_ref, s_ref,
                 o_ref, hi_ref, lo_ref, mean_ref):
    i = pl.program_id(1)
    n_i = pl.num_programs(1)
    x = x_ref[...]
    mod = mod_ref[...]
    g = g_ref[...]
    rows = x.shape[0]
    h = _modulated_norm(x, mod, g[0:1])

    def store_split(first_row, v):
        hi = v.astype(BF16)
        hi_ref[first_row:first_row + v.shape[0], :] = hi
        lo_ref[first_row:first_row + v.shape[0], :] = (v - hi.astype(F32)).astype(BF16)

    store_split(0, _modulated_norm(xp_ref[...], mod, g[0:1]) * (i > 0).astype(F32))
    store_split(POOL_HALO, h)
    store_split(POOL_HALO + rows, _modulated_norm(xn_ref[...], mod, g[0:1])
                * (i < n_i - 1).astype(F32))

    for gi, window in enumerate(POOL_WINDOWS):
        lanes = slice(gi * POOL_GROUP, (gi + 1) * POOL_GROUP)
        band = band_ref[gi]
        for blk in range(rows // POOL_BLOCK):
            src = pl.ds(blk * POOL_BLOCK, POOL_BAND_K)
            tot = (jnp.dot(band, hi_ref[src, lanes], preferred_element_type=F32)
                   + jnp.dot(band, lo_ref[src, lanes], preferred_element_type=F32))
            mean_ref[blk * POOL_BLOCK:(blk + 1) * POOL_BLOCK, lanes] = tot * (1.0 / window)

    @pl.when(i == 0)
    def _():
        mean_ref[0:POOL_HALO, :] = mean_ref[0:POOL_HALO, :] * edge_ref[0]

    @pl.when(i == n_i - 1)
    def _():
        mean_ref[rows - POOL_HALO:rows, :] = mean_ref[rows - POOL_HALO:rows, :] * edge_ref[1]

    diff = (mean_ref[...] - h).astype(BF16)
    ys = [jnp.dot(diff[:, gi * POOL_GROUP:(gi + 1) * POOL_GROUP], w_ref[gi],
                  preferred_element_type=F32) for gi in range(N_POOL_GROUPS)]
    y = (jnp.concatenate(ys, axis=-1) + b_ref[...]) * s_ref[...]
    o_ref[...] = _residual(x, y, mod, g[1:2], 1.0)


def _pool_sublayer(x, mod, sub, g_pair, pool_w, pool_b, pool_scale):
    batch, seq, _ = x.shape
    halo_blocks = POOL_ROWS // POOL_HALO
    n_halo = seq // POOL_HALO
    return pl.pallas_call(
        _pool_kernel,
        grid=(batch, seq // POOL_ROWS),
        in_specs=[
            pl.BlockSpec((None, POOL_ROWS, D_MODEL), lambda b, i: (b, i, 0)),
            pl.BlockSpec((None, POOL_HALO, D_MODEL),
                         lambda b, i: (b, jnp.maximum(i * halo_blocks - 1, 0), 0)),
            pl.BlockSpec((None, POOL_HALO, D_MODEL),
                         lambda b, i: (b, jnp.minimum((i + 1) * halo_blocks, n_halo - 1), 0)),
            pl.BlockSpec((None, None, 3, D_MODEL), lambda b, i: (b, sub, 0, 0)),
            pl.BlockSpec((2, D_MODEL), lambda b, i: (0, 0)),
            pl.BlockSpec((N_POOL_GROUPS, POOL_BLOCK, POOL_BAND_K), lambda b, i: (0, 0, 0)),
            pl.BlockSpec((2, POOL_HALO, D_MODEL), lambda b, i: (0, 0, 0)),
            pl.BlockSpec((N_POOL_GROUPS, POOL_GROUP, POOL_GROUP), lambda b, i: (0, 0, 0)),
            pl.BlockSpec((1, D_MODEL), lambda b, i: (0, 0)),
            pl.BlockSpec((1, D_MODEL), lambda b, i: (0, 0)),
        ],
        out_specs=pl.BlockSpec((None, POOL_ROWS, D_MODEL), lambda b, i: (b, i, 0)),
        out_shape=jax.ShapeDtypeStruct(x.shape, F32),
        scratch_shapes=[
            pltpu.VMEM((POOL_ROWS + 2 * POOL_HALO, D_MODEL), BF16),
            pltpu.VMEM((POOL_ROWS + 2 * POOL_HALO, D_MODEL), BF16),
            pltpu.VMEM((POOL_ROWS, D_MODEL), F32),
        ],
        compiler_params=_params(2),
        name="pool_sublayer",
    )(x, x, x, mod, g_pair, _pool_bands(), _pool_edge_ratios(seq), pool_w.astype(BF16),
      pool_b.reshape(1, D_MODEL), pool_scale.reshape(1, D_MODEL))


LAT_COLS = Q_LORA + KV_LORA + 2 * LANES


def _fold_qk_kernel(wuq_ref, wuk_ref, o_ref):
    o_ref[...] = Q_SCALE * lax.dot_general(
        wuq_ref[:, :QK_NOPE], wuk_ref[...], (((1,), (1,)), ((), ())),
        preferred_element_type=F32, precision=lax.Precision.HIGHEST)


def _mla_weights(w_in, w_uq, w_uk, w_uv, w_o):
    wuq_h = w_uq.transpose(1, 0, 2)
    wuk_h = w_uk.transpose(1, 0, 2)
    w_ql = pl.pallas_call(
        _fold_qk_kernel,
        grid=(N_HEADS,),
        in_specs=[
            pl.BlockSpec((None, Q_LORA, QK_NOPE + QK_ROPE), lambda h: (h, 0, 0)),
            pl.BlockSpec((None, KV_LORA, QK_NOPE), lambda h: (h, 0, 0)),
        ],
        out_specs=pl.BlockSpec((None, Q_LORA, KV_LORA), lambda h: (h, 0, 0)),
        out_shape=jax.ShapeDtypeStruct((N_HEADS, Q_LORA, KV_LORA), F32),
        compiler_params=_params(1),
        name="fold_qk",
    )(wuq_h, wuk_h)

    def rope_cols(w, swap):
        a, b = w[..., :HALF_ROPE], w[..., HALF_ROPE:]
        first, second = (b, a) if swap else (a, b)
        pad = jnp.zeros(w.shape[:-1] + (LANES - QK_ROPE,), w.dtype)
        return jnp.concatenate([first, second, pad], axis=-1)

    wqt = jnp.concatenate([w_ql, wuq_h[..., QK_NOPE:]], axis=-1).transpose(0, 2, 1).astype(BF16)
    wk_rope = w_in[:, Q_LORA + KV_LORA:]
    w_lat = jnp.concatenate(
        [w_in[:, :Q_LORA + KV_LORA], rope_cols(wk_rope, False), rope_cols(wk_rope, True)],
        axis=-1).astype(BF16)
    w_uvt = w_uv.transpose(1, 2, 0).astype(BF16)
    return w_lat, wqt, w_uvt, w_o.astype(BF16)


def _rope_tables(seq):
    inv = 1.0 / (ROPE_THETA ** (np.arange(0, QK_ROPE, 2, dtype=np.float64) / QK_ROPE))
    ang = np.arange(seq, dtype=np.float64)[:, None] * inv[None, :]
    cos, sin = np.cos(ang), np.sin(ang)
    pad = np.zeros((seq, LANES - QK_ROPE))
    as_f32 = lambda a: jnp.asarray(a.astype(np.float32))
    return (as_f32(np.concatenate([cos, cos, pad], axis=-1)),
            as_f32(np.concatenate([-sin, sin, pad], axis=-1)),
            as_f32(cos.T * Q_SCALE), as_f32(sin.T * Q_SCALE))


def _mla_pre_kernel(x_ref, mod_ref, g_ref, wlat_ref, qn_ref, kvn_ref, wqt_ref,
                    cosk_ref, sink_ref, cosq_ref, sinq_ref, qt_ref, k_ref, vt_ref):
    x = x_ref[...]
    rows = x.shape[0]
    hb = _modulated_norm(x, mod_ref[...], g_ref[0:1, :]).astype(BF16)
    lat = jnp.dot(hb, wlat_ref[...], preferred_element_type=F32)
    c_q = _rms(lat[:, :Q_LORA]) * qn_ref[...]
    c_kv = _rms(lat[:, Q_LORA:Q_LORA + KV_LORA]) * kvn_ref[...]
    k_rope = (lat[:, Q_LORA + KV_LORA:Q_LORA + KV_LORA + LANES] * cosk_ref[...]
              + lat[:, Q_LORA + KV_LORA + LANES:] * sink_ref[...])
    k_ref[...] = jnp.concatenate([c_kv, k_rope], axis=-1).astype(BF16)
    vt_ref[0:KV_LORA, :] = c_kv.T.astype(BF16)
    ones_row = lax.broadcasted_iota(jnp.int32, (VT_ROWS - KV_LORA, rows), 0) == 0
    vt_ref[KV_LORA:, :] = ones_row.astype(BF16)
    cqt = c_q.T.astype(BF16)
    cos_q = cosq_ref[...]
    sin_q = sinq_ref[...]
    r1 = KV_LORA + HALF_ROPE
    for head in range(N_HEADS):
        a = jnp.dot(wqt_ref[head], cqt, preferred_element_type=F32)
        x1 = a[KV_LORA:r1]
        x2 = a[r1:]
        qt_ref[head, 0:KV_LORA, :] = a[:KV_LORA].astype(BF16)
        qt_ref[head, KV_LORA:r1, :] = (x1 * cos_q - x2 * sin_q).astype(BF16)
        qt_ref[head, r1:QK_FEATS, :] = (x2 * cos_q + x1 * sin_q).astype(BF16)


def _mla_pre(x, mod, sub, g_pair, w_lat, q_norm, kv_norm, wqt, tables):
    batch, seq, _ = x.shape
    rows = MLA_PRE_ROWS
    return pl.pallas_call(
        _mla_pre_kernel,
        grid=(batch, seq // rows),
        in_specs=[
            pl.BlockSpec((None, rows, D_MODEL), lambda b, i: (b, i, 0)),
            pl.BlockSpec((None, None, 3, D_MODEL), lambda b, i: (b, sub, 0, 0)),
            pl.BlockSpec((2, D_MODEL), lambda b, i: (0, 0)),
            pl.BlockSpec((D_MODEL, LAT_COLS), lambda b, i: (0, 0)),
            pl.BlockSpec((1, Q_LORA), lambda b, i: (0, 0)),
            pl.BlockSpec((1, KV_LORA), lambda b, i: (0, 0)),
            pl.BlockSpec((N_HEADS, QK_FEATS, Q_LORA), lambda b, i: (0, 0, 0)),
            pl.BlockSpec((rows, LANES), lambda b, i: (i, 0)),
            pl.BlockSpec((rows, LANES), lambda b, i: (i, 0)),
            pl.BlockSpec((HALF_ROPE, rows), lambda b, i: (0, i)),
            pl.BlockSpec((HALF_ROPE, rows), lambda b, i: (0, i)),
        ],
        out_specs=[
            pl.BlockSpec((None, N_HEADS, QK_FEATS, rows), lambda b, i: (b, 0, 0, i)),
            pl.BlockSpec((None, rows, QK_WIDTH), lambda b, i: (b, i, 0)),
            pl.BlockSpec((None, VT_ROWS, rows), lambda b, i: (b, 0, i)),
        ],
        out_shape=[
            jax.ShapeDtypeStruct((batch, N_HEADS, QK_FEATS, seq), BF16),
            jax.ShapeDtypeStruct((batch, seq, QK_WIDTH), BF16),
            jax.ShapeDtypeStruct((batch, VT_ROWS, seq), BF16),
        ],
        compiler_params=_params(2),
        name="mla_pre",
    )(x, mod, g_pair, w_lat, q_norm.reshape(1, Q_LORA), kv_norm.reshape(1, KV_LORA),
      wqt, *tables)


def _mla_attn_kernel(qt_ref, qt_ahead_ref, k_ref, vt_ref, x_ref, mod_ref, g_ref, wuvt_ref, wo_ref,
                     o_ref, st_ref, m_ref, acc_ref, ov_ref):
    seq = k_ref.shape[0]
    n_tiles = seq // KEY_TILE

    def score_tile(qt_next, slot, kt, mx):
        keys = pl.ds(kt * KEY_TILE, KEY_TILE)
        st = jnp.dot(k_ref[keys, 0:QK_FEATS], qt_next, preferred_element_type=F32)
        st_ref[slot, keys, :] = st
        tmax = jnp.max(st.reshape(KEY_TILE // SUBLANES, SUBLANES, st.shape[1]), axis=0)
        return tmax if mx is None else jnp.maximum(mx, tmax)

    def prob_tile(slot, kt, m, acc):
        keys = pl.ds(kt * KEY_TILE, KEY_TILE)
        pt = jnp.exp2(st_ref[slot, keys, :] - m).astype(BF16)
        part = jnp.dot(vt_ref[:, keys], pt, preferred_element_type=F32)
        return part if acc is None else acc + part

    def scores(head, slot):
        qt_next = qt_ref[head]
        mx = None
        for kt in range(n_tiles):
            mx = score_tile(qt_next, slot, kt, mx)
        m_ref[slot] = jnp.max(mx, axis=0, keepdims=True)

    def emit(head, slot):
        acc = acc_ref[slot]
        ot = acc[:KV_LORA] * (1.0 / acc[KV_LORA:KV_LORA + 1])
        ovt = jnp.dot(wuvt_ref[head], ot.astype(BF16), preferred_element_type=F32)
        first = head * V_HEAD
        if not isinstance(head, int):
            first = pl.multiple_of(first, V_HEAD)
        ov_ref[pl.ds(first, V_HEAD), :] = ovt.astype(BF16)

    def overlapped(head, slot, emit_previous=True, qt_next=None):
        if qt_next is None:
            qt_next = qt_ref[head + 1]
        m = m_ref[slot]
        acc = None
        mx = None
        for kt in range(n_tiles + SCORE_LEAD):
            if kt < n_tiles:
                mx = score_tile(qt_next, 1 - slot, kt, mx)
            if emit_previous and kt == EMIT_AT:
                emit(head - 1, 1 - slot)
            if kt >= SCORE_LEAD:
                acc = prob_tile(slot, kt - SCORE_LEAD, m, acc)
        m_ref[1 - slot] = jnp.max(mx, axis=0, keepdims=True)
        acc_ref[slot] = acc

    def quad_body(j, carry):
        for r in range(HEADS_PER_TRIP):
            overlapped(HEADS_PER_TRIP * j + 1 + r, (1 + r) % 2)
        return carry

    @pl.when(pl.program_id(1) == 0)
    def _():
        scores(0, 0)

    overlapped(0, 0, emit_previous=False)
    n_trips = (N_HEADS - 2) // HEADS_PER_TRIP
    lax.fori_loop(0, n_trips, quad_body, 0)
    for head in range(1 + n_trips * HEADS_PER_TRIP, N_HEADS - 1):
        overlapped(head, head % 2)
    overlapped(N_HEADS - 1, 1, qt_next=qt_ahead_ref[0])
    emit(N_HEADS - 1, 1)
    y = lax.dot_general(ov_ref[...], wo_ref[...], (((0,), (0,)), ((), ())),
                        preferred_element_type=F32)
    o_ref[...] = _residual(x_ref[...], y, mod_ref[...], g_ref[1:2, :], 1.0)


def _mla_attn(x, qt, k, vt, mod, sub, g_pair, w_uvt, w_ot):
    batch, seq, _ = x.shape
    rows = ATTN_ROWS
    n_q = seq // rows
    return pl.pallas_call(
        _mla_attn_kernel,
        grid=(batch, n_q),
        in_specs=[
            pl.BlockSpec((None, N_HEADS, QK_FEATS, rows), lambda b, i: (b, 0, 0, i)),
            pl.BlockSpec((None, 1, QK_FEATS, rows),
                         lambda b, i: (b, 0, 0, jnp.minimum(i + 1, n_q - 1))),
            pl.BlockSpec((None, seq, QK_WIDTH), lambda b, i: (b, 0, 0)),
            pl.BlockSpec((None, VT_ROWS, seq), lambda b, i: (b, 0, 0)),
            pl.BlockSpec((None, rows, D_MODEL), lambda b, i: (b, i, 0)),
            pl.BlockSpec((None, None, 3, D_MODEL), lambda b, i: (b, sub, 0, 0)),
            pl.BlockSpec((2, D_MODEL), lambda b, i: (0, 0)),
            _resident((N_HEADS, V_HEAD, KV_LORA), lambda b, i: (0, 0, 0)),
            _resident((D_MODEL, N_HEADS * V_HEAD), lambda b, i: (0, 0)),
        ],
        out_specs=pl.BlockSpec((None, rows, D_MODEL), lambda b, i: (b, i, 0)),
        out_shape=jax.ShapeDtypeStruct(x.shape, F32),
        scratch_shapes=[
            pltpu.VMEM((2, seq, rows), F32),
            pltpu.VMEM((2, 1, rows), F32),
            pltpu.VMEM((2, VT_ROWS, rows), F32),
            pltpu.VMEM((N_HEADS * V_HEAD, rows), BF16),
        ],
        compiler_params=_params(2),
        name="mla_attn",
    )(qt, qt, k, vt, x, mod, g_pair, w_uvt, w_ot)


def kernel(x, c, ada_w, ada_b, norm_g, ffn_w_in, ffn_w_out, pool_w, pool_b, pool_scale,
           mla_w_in, mla_q_norm, mla_kv_norm, mla_w_uq, mla_w_uk, mla_w_uv, mla_w_o):
    batch, seq, _ = x.shape
    mods = _ada_mod(c, ada_w, ada_b).reshape(DEPTH, batch, 3, 3, D_MODEL)
    tables = _rope_tables(seq)
    w_in_all = _ffn_w_in_prep(ffn_w_in)
    w_out_all = ffn_w_out.astype(BF16)
    for i in range(DEPTH):
        mod = mods[i]
        g = norm_g[i]
        li = i // 2
        x = _ffn_sublayer(x, mod, 0, g[0:2], w_in_all, w_out_all, i, 0)
        if i % 2 == 0:
            x = _pool_sublayer(x, mod, 1, g[2:4], pool_w[li], pool_b[li], pool_scale[li])
        else:
            w_lat, wqt, w_uvt, w_ot = _mla_weights(
                mla_w_in[li], mla_w_uq[li], mla_w_uk[li], mla_w_uv[li], mla_w_o[li])
            qt, k, vt = _mla_pre(x, mod, 1, g[2:4], w_lat, mla_q_norm[li], mla_kv_norm[li],
                                 wqt, tables)
            x = _mla_attn(x, qt, k, vt, mod, 1, g[2:4], w_uvt, w_ot)
        x = _ffn_sublayer(x, mod, 2, g[4:6], w_in_all, w_out_all, i, 1)
    return x
```
